```python
import jax, jax.numpy as jnp
from jax import lax
import numpy as np

D_MODEL = 2048
BATCH = 8
SEQ = 2048
DEPTH = 1

CHUNK = 64
EPS = 1e-6
D_A = D_MODEL // 2
A_HEADS = 8
A_DK = D_A // A_HEADS
A_DV = D_A // A_HEADS
D_B = D_MODEL // 2
CONV_W = 3
N_BRANCH = 2
IN_SIZES = (D_A, D_A, D_A, D_A, D_B, D_B, D_B, D_B)
IN_COLS = sum(IN_SIZES)

kernel_name = "hgrn2_shortconv_gated_hybrid"


def rms_norm(x, gain):
    xf = x.astype(jnp.float32)
    y = xf * lax.rsqrt(jnp.mean(xf * xf, axis=-1, keepdims=True) + EPS)
    return (y * gain.astype(jnp.float32)).astype(x.dtype)


def hgrn2_chunkwise(q, log_f, k, v):
    bsz, s, h, dk = q.shape
    dv = v.shape[-1]
    n = s // CHUNK
    q, log_f, k = (t.reshape(bsz, n, CHUNK, h, dk) for t in (q, log_f, k))
    v = v.reshape(bsz, n, CHUNK, h, dv)
    b = jnp.cumsum(log_f, axis=2)
    b_mid = b[:, :, CHUNK // 2 - 1:CHUNK // 2]
    b_last = b[:, :, CHUNK - 1:]
    q_rel = q * jnp.exp(b - b_mid)
    k_rel = k * jnp.exp(b_mid - b)
    scores = jnp.einsum('bnthd,bnshd->bnhts', q_rel, k_rel)
    causal = jnp.tril(jnp.ones((CHUNK, CHUNK), dtype=bool))
    scores = jnp.where(causal, scores, 0.0)
    o_intra = jnp.einsum('bnhts,bnshv->bnthv', scores, v)
    d_state = jnp.einsum('bnshd,bnshv->bnhdv', k * jnp.exp(b_last - b), v)
    decay = jnp.exp(b_last[:, :, 0])

    def step(state, inp):
        ds_c, dec_c = inp
        return dec_c[..., None] * state + ds_c, state

    state0 = jnp.zeros((bsz, h, dk, dv), jnp.float32)
    _, state_before = lax.scan(step, state0, (jnp.moveaxis(d_state, 1, 0), jnp.moveaxis(decay, 1, 0)))
    state_before = jnp.moveaxis(state_before, 0, 1)
    o_inter = jnp.einsum('bnthd,bnhdv->bnthv', q * jnp.exp(b), state_before)
    return (o_intra + o_inter).reshape(bsz, s, h, dv)


def causal_depthwise_conv(u, w, bias):
    s = u.shape[1]
    up = jnp.pad(u, ((0, 0), (CONV_W - 1, 0), (0, 0)))
    y = bias
    for j in range(CONV_W):
        y = y + w[j] * up[:, j:j + s]
    return y


def setup_inputs(seed: int = 0) -> dict:
    key = jax.random.key(seed)
    ks = jax.random.split(key, 16)
    f32 = jnp.float32
    nrm = lambda k, shape, scale: (jax.random.normal(k, shape, f32) * scale)
    return {
        "x": nrm(ks[0], (BATCH, SEQ, D_MODEL), 1.0),
        "c": nrm(ks[1], (BATCH, D_MODEL), 1.0),
        "w_ada": nrm(ks[2], (DEPTH, D_MODEL, 3 * D_MODEL), D_MODEL ** -0.5),
        "b_ada": nrm(ks[3], (DEPTH, 3 * D_MODEL), 0.02),
        "g_pre": 1.0 + nrm(ks[4], (DEPTH, D_MODEL), 0.02),
        "w_in": nrm(ks[5], (DEPTH, D_MODEL, IN_COLS), D_MODEL ** -0.5),
        "lb_logits": nrm(ks[6], (DEPTH + 1, D_A), 0.1),
        "g_head_a": 1.0 + nrm(ks[7], (DEPTH, A_DV), 0.02),
        "conv_w": nrm(ks[8], (DEPTH, CONV_W, D_B), CONV_W ** -0.5),
        "conv_b": nrm(ks[9], (DEPTH, D_B), 0.02),
        "w_up_a": nrm(ks[10], (DEPTH, D_A, D_MODEL), D_A ** -0.5),
        "w_up_b": nrm(ks[11], (DEPTH, D_B, D_MODEL), D_B ** -0.5),
        "w_merge": nrm(ks[12], (DEPTH, D_MODEL, N_BRANCH * D_MODEL), D_MODEL ** -0.5),
        "b_merge": nrm(ks[13], (DEPTH, N_BRANCH * D_MODEL), 0.02),
        "w_o": nrm(ks[14], (DEPTH, D_MODEL, D_MODEL), D_MODEL ** -0.5),
        "g_post": 1.0 + nrm(ks[15], (DEPTH, D_MODEL), 0.02),
    }


def reference(x, c, w_ada, b_ada, g_pre, w_in, lb_logits, g_head_a, conv_w, conv_b,
              w_up_a, w_up_b, w_merge, b_merge, w_o, g_post):
    bsz, s, _ = x.shape
    split_at = [int(v) for v in np.cumsum(IN_SIZES)[:-1]]
    lb_all = jnp.cumsum(jax.nn.softmax(lb_logits.astype(jnp.float32), axis=0), axis=0)
    c_act = jax.nn.silu(c)
    for l in range(DEPTH):
        mod = jnp.einsum('bd,de->be', c_act, w_ada[l]) + b_ada[l]
        shift, scale, gate = (m[:, None, :] for m in jnp.split(mod, 3, axis=-1))
        h = rms_norm(x, g_pre[l]) * (1.0 + scale) + shift

        proj = jnp.einsum('bsd,de->bse', h, w_in[l])
        q_a, f_a, i_a, z_a, gate_bb, gate_cc, v_b, z_b = jnp.split(proj, split_at, axis=-1)

        lb = lb_all[l]
        f = lb + (1.0 - lb) * jax.nn.sigmoid(f_a.astype(jnp.float32))
        log_f = jnp.log(f)
        k_a = 1.0 - f
        heads = lambda t: t.reshape(bsz, s, A_HEADS, -1)
        o_a = hgrn2_chunkwise(heads(q_a.astype(jnp.float32)), heads(log_f), heads(k_a),
                              heads(i_a.astype(jnp.float32)))
        o_a = rms_norm(o_a, g_head_a[l]).reshape(bsz, s, D_A).astype(x.dtype)
        y_a = o_a * jax.nn.silu(z_a)

        conv_out = causal_depthwise_conv(gate_cc * v_b, conv_w[l], conv_b[l])
        y_b = gate_bb * conv_out * jax.nn.silu(z_b)

        p_a = jnp.einsum('bse,ed->bsd', y_a, w_up_a[l])
        p_b = jnp.einsum('bse,ed->bsd', y_b, w_up_b[l])
        merge = jax.nn.sigmoid(jnp.einsum('bsd,de->bse', h, w_merge[l]) + b_merge[l])
        m_a, m_b = jnp.split(merge, N_BRANCH, axis=-1)
        out = jnp.einsum('bsd,de->bse', m_a * p_a + m_b * p_b, w_o[l])

        x = x + gate * rms_norm(out, g_post[l])
    return x
```

```python
import functools

import jax
import jax.numpy as jnp
from jax import lax
from jax.experimental import pallas as pl
from jax.experimental.pallas import tpu as pltpu

F32 = jnp.float32
BF16 = jnp.bfloat16

EPS = 1e-6
CHUNK = 64
A_HEADS = 8
CONV_W = 3

VMEM_LIMIT_BYTES = 56 * 1024 * 1024


def _sigmoid(v):
    return 1.0 / (1.0 + jnp.exp(-v))


def _mod_kernel(c_ref, w_ref, b_ref, o_ref):
    c = c_ref[...]
    c_act = c * _sigmoid(c)
    acc = jnp.dot(c_act.astype(BF16), w_ref[...].astype(BF16), preferred_element_type=F32)
    o_ref[...] = acc + b_ref[...]


def _mod_call(c, w_ada, b_ada, tn=1024):
    bsz, d = c.shape
    n = w_ada.shape[1]
    return pl.pallas_call(
        _mod_kernel,
        grid=(n // tn,),
        in_specs=[
            pl.BlockSpec((bsz, d), lambda j: (0, 0)),
            pl.BlockSpec((d, tn), lambda j: (0, j)),
            pl.BlockSpec((1, tn), lambda j: (0, j)),
        ],
        out_specs=pl.BlockSpec((bsz, tn), lambda j: (0, j)),
        out_shape=jax.ShapeDtypeStruct((bsz, n), F32),
        compiler_params=pltpu.CompilerParams(
            dimension_semantics=("arbitrary",), vmem_limit_bytes=VMEM_LIMIT_BYTES),
        name="adaln_mod",
    )(c, w_ada, b_ada.reshape(1, n))


def _proj_kernel(x_ref, mod_ref, g_ref, w_ref, b_ref, o_ref, h_ref, *, n_proj_tiles, rows_per_iter):
    j = pl.program_id(1)
    tm, d = x_ref.shape

    @pl.when(j == 0)
    def _():
        shift = mod_ref[0, 0:1, :]
        scale1 = 1.0 + mod_ref[0, 1:2, :]
        g = g_ref[...]

        def body(r, carry):
            r0 = pl.multiple_of(r * rows_per_iter, rows_per_iter)
            xv = x_ref[pl.ds(r0, rows_per_iter), :]
            ms = jnp.mean(xv * xv, axis=-1, keepdims=True)
            y = xv * lax.rsqrt(ms + EPS) * g
            h_ref[pl.ds(r0, rows_per_iter), :] = (y * scale1 + shift).astype(BF16)
            return carry

        lax.fori_loop(0, tm // rows_per_iter, body, 0)

    @pl.when(j < n_proj_tiles)
    def _():
        acc = jnp.dot(h_ref[...], w_ref[...], preferred_element_type=F32)
        o_ref[...] = acc.astype(BF16)

    @pl.when(j >= n_proj_tiles)
    def _():
        acc = jnp.dot(h_ref[...], w_ref[...], preferred_element_type=F32)
        o_ref[...] = _sigmoid(acc + b_ref[...]).astype(BF16)


def _proj_call(x2, mod3, g_pre, w_cat, b_cat, seq, n_proj_cols, tm=1024, tn=1024):
    t, d = x2.shape
    n = w_cat.shape[1]
    kern = functools.partial(_proj_kernel, n_proj_tiles=n_proj_cols // tn, rows_per_iter=32)
    return pl.pallas_call(
        kern,
        grid=(t // tm, n // tn),
        in_specs=[
            pl.BlockSpec((tm, d), lambda i, j: (i, 0)),
            pl.BlockSpec((1, 3, d), lambda i, j: ((i * tm) // seq, 0, 0)),
            pl.BlockSpec((1, d), lambda i, j: (0, 0)),
            pl.BlockSpec((d, tn), lambda i, j: (0, j)),
            pl.BlockSpec((1, tn), lambda i, j: (0, j)),
        ],
        out_specs=pl.BlockSpec((tm, tn), lambda i, j: (i, j)),
        out_shape=jax.ShapeDtypeStruct((t, n), BF16),
        scratch_shapes=[pltpu.VMEM((tm, d), BF16)],
        compiler_params=pltpu.CompilerParams(
            dimension_semantics=("arbitrary", "arbitrary"), vmem_limit_bytes=VMEM_LIMIT_BYTES),
        name="prenorm_proj",
    )(x2, mod3, g_pre, w_cat, b_cat)


def _mix_kernel(p_ref, lbl_ref, gh_ref, cw_ref, cb_ref, y_ref, state_ref, u_ref, *, d_a, d_b, layer):
    s_idx = pl.program_id(1)
    ts = p_ref.shape[0]
    dk = d_a // A_HEADS
    pad = 8

    @pl.when(s_idx == 0)
    def _():
        state_ref[...] = jnp.zeros_like(state_ref)
        u_ref[0:pad, :] = jnp.zeros((pad, d_b), F32)

    lbl = lbl_ref[...]
    lmax = jnp.max(lbl, axis=0, keepdims=True)
    le = jnp.exp(lbl - lmax)
    lb = (jnp.sum(le[0:layer + 1, :], axis=0, keepdims=True)
          / jnp.sum(le, axis=0, keepdims=True))

    row = lax.broadcasted_iota(jnp.int32, (CHUNK, CHUNK), 0)
    col = lax.broadcasted_iota(jnp.int32, (CHUNK, CHUNK), 1)
    causal = col <= row
    tril = causal.astype(BF16)
    g_head = gh_ref[...]

    for c in range(ts // CHUNK):
        r0 = c * CHUNK
        f_a = p_ref[r0:r0 + CHUNK, d_a:2 * d_a].astype(F32)
        f = lb + (1.0 - lb) * _sigmoid(f_a)
        log_f = jnp.log(f)
        k = 1.0 - f
        lf_hi = log_f.astype(BF16)
        lf_lo = (log_f - lf_hi.astype(F32)).astype(BF16)
        b = (jnp.dot(tril, lf_hi, preferred_element_type=F32)
             + jnp.dot(tril, lf_lo, preferred_element_type=F32))
        b_mid = b[CHUNK // 2 - 1:CHUNK // 2, :]
        b_last = b[CHUNK - 1:CHUNK, :]
        q = p_ref[r0:r0 + CHUNK, 0:d_a].astype(F32)
        q_rel = q * jnp.exp(b - b_mid)
        k_rel = k * jnp.exp(b_mid - b)
        q_abs = (q * jnp.exp(b)).astype(BF16)
        k_last = (k * jnp.exp(b_last - b)).astype(BF16)
        decay = jnp.exp(b_last)
        q_rel = q_rel.astype(BF16)
        k_rel = k_rel.astype(BF16)
        for h in range(A_HEADS):
            hs = slice(h * dk, (h + 1) * dk)
            v = p_ref[r0:r0 + CHUNK, 2 * d_a + h * dk:2 * d_a + (h + 1) * dk]
            scores = lax.dot_general(q_rel[:, hs], k_rel[:, hs], (((1,), (1,)), ((), ())),
                                     preferred_element_type=F32)
            scores = jnp.where(causal, scores, 0.0).astype(BF16)
            o = jnp.dot(scores, v, preferred_element_type=F32)
            st = state_ref[h]
            o = o + lax.dot_general(q_abs[:, hs], st.astype(BF16), (((1,), (1,)), ((), ())),
                                    preferred_element_type=F32)
            d_st = lax.dot_general(v, k_last[:, hs], (((0,), (0,)), ((), ())),
                                   preferred_element_type=F32)
            state_ref[h] = st * decay[:, hs] + d_st
            o = o * lax.rsqrt(jnp.mean(o * o, axis=-1, keepdims=True) + EPS) * g_head
            z = p_ref[r0:r0 + CHUNK, 3 * d_a + h * dk:3 * d_a + (h + 1) * dk].astype(F32)
            y_ref[r0:r0 + CHUNK, h * dk:(h + 1) * dk] = (o * (z * _sigmoid(z))).astype(BF16)

    base = 4 * d_a
    gate_cc = p_ref[:, base + d_b:base + 2 * d_b].astype(F32)
    v_b = p_ref[:, base + 2 * d_b:base + 3 * d_b].astype(F32)
    u_ref[pad:pad + ts, :] = gate_cc * v_b
    conv = cb_ref[...] + cw_ref[CONV_W - 1:CONV_W, :] * u_ref[pad:pad + ts, :]
    for jj in range(CONV_W - 1):
        back = CONV_W - 1 - jj
        conv = conv + cw_ref[jj:jj + 1, :] * u_ref[pad - back:pad - back + ts, :]
    u_ref[0:pad, :] = u_ref[ts:ts + pad, :]
    gate_bb = p_ref[:, base:base + d_b].astype(F32)
    z_b = p_ref[:, base + 3 * d_b:base + 4 * d_b].astype(F32)
    y_ref[:, d_a:d_a + d_b] = (gate_bb * conv * (z_b * _sigmoid(z_b))).astype(BF16)


def _mix_call(pcat, lb_logits, g_head, conv_w, conv_b, bsz, seq, d_a, d_b, layer, ts=256):
    n_in = 4 * d_a + 4 * d_b
    dk = d_a // A_HEADS
    spb = seq // ts
    kern = functools.partial(_mix_kernel, d_a=d_a, d_b=d_b, layer=layer)
    return pl.pallas_call(
        kern,
        grid=(bsz, spb),
        in_specs=[
            pl.BlockSpec((ts, n_in), lambda b, s: (b * spb + s, 0)),
            pl.BlockSpec(lb_logits.shape, lambda b, s: (0, 0)),
            pl.BlockSpec((1, dk), lambda b, s: (0, 0)),
            pl.BlockSpec((CONV_W, d_b), lambda b, s: (0, 0)),
            pl.BlockSpec((1, d_b), lambda b, s: (0, 0)),
        ],
        out_specs=pl.BlockSpec((ts, d_a + d_b), lambda b, s: (b * spb + s, 0)),
        out_shape=jax.ShapeDtypeStruct((bsz * seq, d_a + d_b), BF16),
        scratch_shapes=[pltpu.VMEM((A_HEADS, dk, dk), F32), pltpu.VMEM((ts + 8, d_b), F32)],
        compiler_params=pltpu.CompilerParams(
            dimension_semantics=("arbitrary", "arbitrary"), vmem_limit_bytes=VMEM_LIMIT_BYTES),
        name="hgrn2_conv_mix",
    )(pcat, lb_logits, g_head, conv_w, conv_b)


def _out_kernel(y_ref, m_ref, x_ref, mod_ref, wa_ref, wb_ref, wo_ref, g_ref, o_ref, *, d_a):
    d = x_ref.shape[1]
    p_a = jnp.dot(y_ref[:, 0:d_a], wa_ref[...], preferred_element_type=F32)
    p_b = jnp.dot(y_ref[:, d_a:], wb_ref[...], preferred_element_type=F32)
    mix = m_ref[:, 0:d].astype(F32) * p_a + m_ref[:, d:].astype(F32) * p_b
    out = jnp.dot(mix.astype(BF16), wo_ref[...], preferred_element_type=F32)
    normed = out * lax.rsqrt(jnp.mean(out * out, axis=-1, keepdims=True) + EPS) * g_ref[...]
    o_ref[...] = x_ref[...] + mod_ref[0, 2:3, :] * normed


def _out_call(y, pcat, x2, mod3, w_up_a, w_up_b, w_o, g_post, seq, m_col_block, tm=256):
    t, d = x2.shape
    d_a = w_up_a.shape[0]
    d_b = w_up_b.shape[0]
    kern = functools.partial(_out_kernel, d_a=d_a)
    const = lambda i: (0, 0)
    return pl.pallas_call(
        kern,
        grid=(t // tm,),
        in_specs=[
            pl.BlockSpec((tm, d_a + d_b), lambda i: (i, 0)),
            pl.BlockSpec((tm, 2 * d), lambda i: (i, m_col_block)),
            pl.BlockSpec((tm, d), lambda i: (i, 0)),
            pl.BlockSpec((1, 3, d), lambda i: ((i * tm) // seq, 0, 0)),
            pl.BlockSpec((d_a, d), const, pipeline_mode=pl.Buffered(1)),
            pl.BlockSpec((d_b, d), const, pipeline_mode=pl.Buffered(1)),
            pl.BlockSpec((d, d), const, pipeline_mode=pl.Buffered(1)),
            pl.BlockSpec((1, d), const),
        ],
        out_specs=pl.BlockSpec((tm, d), lambda i: (i, 0)),
        out_shape=jax.ShapeDtypeStruct((t, d), F32),
        compiler_params=pltpu.CompilerParams(
            dimension_semantics=("arbitrary",), vmem_limit_bytes=VMEM_LIMIT_BYTES),
        name="merge_out",
    )(y, pcat, x2, mod3, w_up_a, w_up_b, w_o, g_post)


def kernel(x, c, w_ada, b_ada, g_pre, w_in, lb_logits, g_head_a, conv_w, conv_b,
           w_up_a, w_up_b, w_merge, b_merge, w_o, g_post):
    bsz, seq, d = x.shape
    depth = w_in.shape[0]
    d_a = w_up_a.shape[1]
    d_b = w_up_b.shape[1]
    n_proj = w_in.shape[2]
    assert n_proj == 4 * d_a + 4 * d_b and w_merge.shape[2] == 2 * d
    assert n_proj % (2 * d) == 0 and seq % CHUNK == 0
    x2 = x.reshape(bsz * seq, d)
    for l in range(depth):
        mod3 = _mod_call(c, w_ada[l], b_ada[l]).reshape(bsz, 3, d)
        w_cat = jnp.concatenate([w_in[l], w_merge[l]], axis=1).astype(BF16)
        b_cat = jnp.concatenate([jnp.zeros((n_proj,), F32), b_merge[l]]).reshape(1, -1)
        pcat = _proj_call(x2, mod3, g_pre[l:l + 1], w_cat, b_cat, seq, n_proj)
        y = _mix_call(pcat, lb_logits, g_head_a[l:l + 1], conv_w[l], conv_b[l:l + 1],
                      bsz, seq, d_a, d_b, l)
        x2 = _out_call(y, pcat, x2, mod3, w_up_a[l].astype(BF16), w_up_b[l].astype(BF16),
                       w_o[l].astype(BF16), g_post[l:l + 1], seq, n_proj // (2 * d))
    return x2.reshape(bsz, seq, d)
```

```python
import functools

import jax
import jax.numpy as jnp
from jax import lax
from jax.experimental import pallas as pl
from jax.experimental.pallas import tpu as pltpu

F32 = jnp.float32
BF16 = jnp.bfloat16

EPS = 1e-6
CHUNK = 64
A_HEADS = 8
CONV_W = 3
CONV_PAD = 8

VMEM_LIMIT_BYTES = 56 * 1024 * 1024


def _sigmoid(v):
    return 1.0 / (1.0 + jnp.exp(-v))


def _mod_kernel(c_ref, w_ref, b_ref, o_ref):
    c = c_ref[...]
    c_act = c * _sigmoid(c)
    acc = jnp.dot(c_act.astype(BF16), w_ref[...].astype(BF16), preferred_element_type=F32)
    o_ref[...] = acc + b_ref[...]


def _mod_call(c, w_ada, b_ada, tn=1024):
    bsz, d = c.shape
    n = w_ada.shape[1]
    return pl.pallas_call(
        _mod_kernel,
        grid=(n // tn,),
        in_specs=[
            pl.BlockSpec((bsz, d), lambda j: (0, 0)),
            pl.BlockSpec((d, tn), lambda j: (0, j)),
            pl.BlockSpec((1, tn), lambda j: (0, j)),
        ],
        out_specs=pl.BlockSpec((bsz, tn), lambda j: (0, j)),
        out_shape=jax.ShapeDtypeStruct((bsz, n), F32),
        compiler_params=pltpu.CompilerParams(
            dimension_semantics=("arbitrary",), vmem_limit_bytes=VMEM_LIMIT_BYTES),
        name="adaln_mod",
    )(c, w_ada, b_ada.reshape(1, n))


def _proj_kernel(x_ref, mod_ref, g_ref, w_ref, b_ref, o_ref, h_ref, *, n_proj_tiles, rows_per_iter):
    j = pl.program_id(1)
    tm, d = x_ref.shape

    @pl.when(j == 0)
    def _():
        shift = mod_ref[0, 0:1, :]
        scale1 = 1.0 + mod_ref[0, 1:2, :]
        g = g_ref[...]

        def body(r, carry):
            r0 = pl.multiple_of(r * rows_per_iter, rows_per_iter)
            xv = x_ref[pl.ds(r0, rows_per_iter), :]
            ms = jnp.mean(xv * xv, axis=-1, keepdims=True)
            y = xv * lax.rsqrt(ms + EPS) * g
            h_ref[pl.ds(r0, rows_per_iter), :] = (y * scale1 + shift).astype(BF16)
            return carry

        lax.fori_loop(0, tm // rows_per_iter, body, 0)

    @pl.when(j < n_proj_tiles)
    def _():
        acc = jnp.dot(h_ref[...], w_ref[...], preferred_element_type=F32)
        o_ref[...] = acc.astype(BF16)

    @pl.when(j >= n_proj_tiles)
    def _():
        acc = jnp.dot(h_ref[...], w_ref[...], preferred_element_type=F32)
        o_ref[...] = _sigmoid(acc + b_ref[...]).astype(BF16)


def _proj_call(x2, mod3, g_pre, w_cat, b_cat, seq, n_proj_cols, tm=1024, tn=1024):
    t, d = x2.shape
    n = w_cat.shape[1]
    kern = functools.partial(_proj_kernel, n_proj_tiles=n_proj_cols // tn, rows_per_iter=32)
    return pl.pallas_call(
        kern,
        grid=(t // tm, n // tn),
        in_specs=[
            pl.BlockSpec((tm, d), lambda i, j: (i, 0)),
            pl.BlockSpec((1, 3, d), lambda i, j: ((i * tm) // seq, 0, 0)),
            pl.BlockSpec((1, d), lambda i, j: (0, 0)),
            pl.BlockSpec((d, tn), lambda i, j: (0, j)),
            pl.BlockSpec((1, tn), lambda i, j: (0, j)),
        ],
        out_specs=pl.BlockSpec((tm, tn), lambda i, j: (i, j)),
        out_shape=jax.ShapeDtypeStruct((t, n), BF16),
        scratch_shapes=[pltpu.VMEM((tm, d), BF16)],
        compiler_params=pltpu.CompilerParams(
            dimension_semantics=("arbitrary", "arbitrary"), vmem_limit_bytes=VMEM_LIMIT_BYTES),
        name="prenorm_proj",
    )(x2, mod3, g_pre, w_cat, b_cat)


def _lower_bound(lbl_ref, layer):
    lbl = lbl_ref[...]
    lmax = jnp.max(lbl, axis=0, keepdims=True)
    le = jnp.exp(lbl - lmax)
    return (jnp.sum(le[0:layer + 1, :], axis=0, keepdims=True)
            / jnp.sum(le, axis=0, keepdims=True))


def _causal_mask():
    row = lax.broadcasted_iota(jnp.int32, (CHUNK, CHUNK), 0)
    col = lax.broadcasted_iota(jnp.int32, (CHUNK, CHUNK), 1)
    return col <= row


def _mix_unit(c, h, p_ref, lb, gh_ref, cw_ref, cb_ref, y_ref, state_ref, u_ref, *, d_a, d_b):
    r0 = c * CHUNK
    rows = slice(r0, r0 + CHUNK)
    dk = d_a // A_HEADS
    hs = slice(h * dk, (h + 1) * dk)
    causal = _causal_mask()
    tril = causal.astype(BF16)

    f_a = p_ref[rows, d_a + h * dk:d_a + (h + 1) * dk].astype(F32)
    lbh = lb[:, hs]
    f = lbh + (1.0 - lbh) * _sigmoid(f_a)
    log_f = jnp.log(f)
    k = 1.0 - f
    lf_hi = log_f.astype(BF16)
    lf_lo = (log_f - lf_hi.astype(F32)).astype(BF16)
    b2 = jnp.dot(tril, jnp.concatenate([lf_hi, lf_lo], axis=1), preferred_element_type=F32)

    base = 4 * d_a
    pad = CONV_PAD
    gate_cc = p_ref[rows, base + d_b + h * dk:base + d_b + (h + 1) * dk].astype(F32)
    v_b = p_ref[rows, base + 2 * d_b + h * dk:base + 2 * d_b + (h + 1) * dk].astype(F32)
    u_ref[pad + r0:pad + r0 + CHUNK, hs] = gate_cc * v_b
    conv = cb_ref[:, hs] + cw_ref[CONV_W - 1:CONV_W, hs] * u_ref[pad + r0:pad + r0 + CHUNK, hs]
    for jj in range(CONV_W - 1):
        back = CONV_W - 1 - jj
        conv = conv + cw_ref[jj:jj + 1, hs] * u_ref[pad + r0 - back:pad + r0 - back + CHUNK, hs]
    gate_bb = p_ref[rows, base + h * dk:base + (h + 1) * dk].astype(F32)
    z_b = p_ref[rows, base + 3 * d_b + h * dk:base + 3 * d_b + (h + 1) * dk].astype(F32)
    y_ref[rows, d_a + h * dk:d_a + (h + 1) * dk] = (
        gate_bb * conv * (z_b * _sigmoid(z_b))).astype(BF16)
    yield

    b = b2[:, 0:dk] + b2[:, dk:2 * dk]
    b_mid = b[CHUNK // 2 - 1:CHUNK // 2, :]
    b_last = b[CHUNK - 1:CHUNK, :]
    q = p_ref[rows, h * dk:(h + 1) * dk].astype(F32)
    q_rel = q * jnp.exp(b - b_mid)
    k_rel = k * jnp.exp(b_mid - b)
    q_abs = (q_rel * jnp.exp(b_mid)).astype(BF16)
    k_last = (k_rel * jnp.exp(b_last - b_mid)).astype(BF16)
    decay = jnp.exp(b_last)
    v = p_ref[rows, 2 * d_a + h * dk:2 * d_a + (h + 1) * dk]
    scores = lax.dot_general(q_rel.astype(BF16), k_rel.astype(BF16), (((1,), (1,)), ((), ())),
                             preferred_element_type=F32)
    d_st = lax.dot_general(v, k_last, (((0,), (0,)), ((), ())),
                           preferred_element_type=F32)
    yield

    scores = jnp.where(causal, scores, 0.0).astype(BF16)
    st = state_ref[h]
    o = jnp.dot(scores, v, preferred_element_type=F32)
    o = o + lax.dot_general(q_abs, st.astype(BF16), (((1,), (1,)), ((), ())),
                            preferred_element_type=F32)
    state_ref[h] = st * decay + d_st
    yield

    o = o * lax.rsqrt(jnp.mean(o * o, axis=-1, keepdims=True) + EPS) * gh_ref[...]
    z = p_ref[rows, 3 * d_a + h * dk:3 * d_a + (h + 1) * dk].astype(F32)
    y_ref[rows, hs] = (o * (z * _sigmoid(z))).astype(BF16)


MIX_UNIT_PHASES = 4


def _merge_piece(j, nc, y_ref, m_ref, wa_ref, wb_ref, mix_ref, *, d_a, d):
    cols = slice(j * nc, (j + 1) * nc)
    p_a = jnp.dot(y_ref[:, 0:d_a], wa_ref[:, cols], preferred_element_type=F32)
    p_b = jnp.dot(y_ref[:, d_a:], wb_ref[:, cols], preferred_element_type=F32)
    m_a = m_ref[:, j * nc:(j + 1) * nc].astype(F32)
    m_b = m_ref[:, d + j * nc:d + (j + 1) * nc].astype(F32)
    mix_ref[:, cols] = (m_a * p_a + m_b * p_b).astype(BF16)


def _wo_piece(j, nc, mix_ref, wo_ref, acc_ref):
    cols = slice(j * nc, (j + 1) * nc)
    acc_ref[:, cols] = jnp.dot(mix_ref[...], wo_ref[:, cols], preferred_element_type=F32)


def _post_piece(acc_ref, x_ref, mod_ref, g_ref, o_ref):
    out = acc_ref[...]
    normed = out * lax.rsqrt(jnp.mean(out * out, axis=-1, keepdims=True) + EPS) * g_ref[...]
    o_ref[...] = x_ref[...] + mod_ref[0, 2:3, :] * normed


def _mix_out_kernel(p_ref, lbl_ref, gh_ref, cw_ref, cb_ref, m_ref, x_ref, mod_ref, wa_ref, wb_ref,
                    wo_ref, g_ref, o_ref, state_ref, u_ref, y0_ref, y1_ref, mix_ref, acc_ref,
                    *, d_a, d_b, layer, n_blocks, blocks_per_seq, out_cols):
    k = pl.program_id(0)
    blk = jnp.minimum(k, n_blocks - 1)
    ts, d = x_ref.shape
    n_chunks = ts // CHUNK
    n_out = d // out_cols

    @pl.when(k == 0)
    def _():
        y1_ref[...] = jnp.zeros_like(y1_ref)

    @pl.when(blk % blocks_per_seq == 0)
    def _():
        state_ref[...] = jnp.zeros_like(state_ref)
        u_ref[0:CONV_PAD, :] = jnp.zeros((CONV_PAD, d_b), F32)

    def step(y_write, y_read):
        lb = _lower_bound(lbl_ref, layer)
        out_pieces = (
            [functools.partial(_merge_piece, j, out_cols, y_read, m_ref, wa_ref, wb_ref, mix_ref,
                               d_a=d_a, d=d) for j in range(n_out)]
            + [functools.partial(_wo_piece, j, out_cols, mix_ref, wo_ref, acc_ref)
               for j in range(n_out)])
        mix_units = [[_mix_unit(c, h, p_ref, lb, gh_ref, cw_ref, cb_ref, y_write, state_ref, u_ref,
                                d_a=d_a, d_b=d_b) for h in range(A_HEADS)]
                     for c in range(n_chunks)]
        groups = [t - ph for t in range(n_chunks + MIX_UNIT_PHASES - 1)
                  for ph in reversed(range(MIX_UNIT_PHASES)) if 0 <= t - ph < n_chunks]
        for gi, c in enumerate(groups):
            lo = (gi * len(out_pieces)) // len(groups)
            hi = ((gi + 1) * len(out_pieces)) // len(groups)
            for piece in out_pieces[lo:hi]:
                piece()
            for unit in mix_units[c]:
                next(unit, None)
        u_ref[0:CONV_PAD, :] = u_ref[ts:ts + CONV_PAD, :]
        _post_piece(acc_ref, x_ref, mod_ref, g_ref, o_ref)

    @pl.when(k % 2 == 0)
    def _():
        step(y0_ref, y1_ref)

    @pl.when(k % 2 == 1)
    def _():
        step(y1_ref, y0_ref)


def _mix_out_call(pcat, lb_logits, g_head, conv_w, conv_b, x2, mod3, w_up_a, w_up_b, w_o, g_post,
                  seq, layer, ts=256):
    t, d = x2.shape
    d_a = w_up_a.shape[0]
    d_b = w_up_b.shape[0]
    n_in = 4 * d_a + 4 * d_b
    dk = d_a // A_HEADS
    n_blocks = t // ts
    m_col_block = n_in // (2 * d)
    kern = functools.partial(_mix_out_kernel, d_a=d_a, d_b=d_b, layer=layer, n_blocks=n_blocks,
                             blocks_per_seq=seq // ts, out_cols=512)
    const = lambda k: (0, 0)
    cur = lambda k: jnp.minimum(k, n_blocks - 1)
    prev = lambda k: jnp.maximum(k - 1, 0)
    return pl.pallas_call(
        kern,
        grid=(n_blocks + 1,),
        in_specs=[
            pl.BlockSpec((ts, n_in), lambda k: (cur(k), 0)),
            pl.BlockSpec(lb_logits.shape, const),
            pl.BlockSpec((1, dk), const),
            pl.BlockSpec((CONV_W, d_b), const),
            pl.BlockSpec((1, d_b), const),
            pl.BlockSpec((ts, 2 * d), lambda k: (prev(k), m_col_block)),
            pl.BlockSpec((ts, d), lambda k: (prev(k), 0)),
            pl.BlockSpec((1, 3, d), lambda k: ((prev(k) * ts) // seq, 0, 0)),
            pl.BlockSpec((d_a, d), const, pipeline_mode=pl.Buffered(1)),
            pl.BlockSpec((d_b, d), const, pipeline_mode=pl.Buffered(1)),
            pl.BlockSpec((d, d), const, pipeline_mode=pl.Buffered(1)),
            pl.BlockSpec((1, d), const),
        ],
        out_specs=pl.BlockSpec((ts, d), lambda k: (prev(k), 0)),
        out_shape=jax.ShapeDtypeStruct((t, d), F32),
        scratch_shapes=[
            pltpu.VMEM((A_HEADS, dk, dk), F32),
            pltpu.VMEM((ts + CONV_PAD, d_b), F32),
            pltpu.VMEM((ts, d_a + d_b), BF16),
            pltpu.VMEM((ts, d_a + d_b), BF16),
            pltpu.VMEM((ts, d), BF16),
            pltpu.VMEM((ts, d), F32),
        ],
        compiler_params=pltpu.CompilerParams(
            dimension_semantics=("arbitrary",), vmem_limit_bytes=VMEM_LIMIT_BYTES),
        name="mix_merge_out",
    )(pcat, lb_logits, g_head, conv_w, conv_b, pcat, x2, mod3, w_up_a, w_up_b, w_o, g_post)


def kernel(x, c, w_ada, b_ada, g_pre, w_in, lb_logits, g_head_a, conv_w, conv_b,
           w_up_a, w_up_b, w_merge, b_merge, w_o, g_post):
    bsz, seq, d = x.shape
    depth = w_in.shape[0]
    d_a = w_up_a.shape[1]
    d_b = w_up_b.shape[1]
    n_proj = w_in.shape[2]
    assert n_proj == 4 * d_a + 4 * d_b and w_merge.shape[2] == 2 * d
    assert n_proj % (2 * d) == 0 and seq % CHUNK == 0
    x2 = x.reshape(bsz * seq, d)
    for l in range(depth):
        mod3 = _mod_call(c, w_ada[l], b_ada[l]).reshape(bsz, 3, d)
        w_cat = jnp.concatenate([w_in[l], w_merge[l]], axis=1).astype(BF16)
        b_cat = jnp.concatenate([jnp.zeros((n_proj,), F32), b_merge[l]]).reshape(1, -1)
        pcat = _proj_call(x2, mod3, g_pre[l:l + 1], w_cat, b_cat, seq, n_proj)
        x2 = _mix_out_call(pcat, lb_logits, g_head_a[l:l + 1], conv_w[l], conv_b[l:l + 1], x2, mod3,
                           w_up_a[l].astype(BF16), w_up_b[l].astype(BF16), w_o[l].astype(BF16),
                           g_post[l:l + 1], seq, l)
    return x2.reshape(bsz, seq, d)
```

```python
import functools

import jax
import jax.numpy as jnp
from jax import lax
from jax.experimental import pallas as pl
from jax.experimental.pallas import tpu as pltpu

F32 = jnp.float32
BF16 = jnp.bfloat16

EPS = 1e-6
CHUNK = 64
A_HEADS = 8
CONV_W = 3
CONV_PAD = 8

VMEM_LIMIT_BYTES = 56 * 1024 * 1024


def _sigmoid(v):
    return 1.0 / (1.0 + jnp.exp(-v))


def _mod_kernel(c_ref, w_ref, b_ref, o_ref):
    c = c_ref[...]
    c_act = c * _sigmoid(c)
    acc = jnp.dot(c_act.astype(BF16), w_ref[...].astype(BF16), preferred_element_type=F32)
    o_ref[...] = acc + b_ref[...]


def _mod_call(c, w_ada, b_ada, tn=1024):
    bsz, d = c.shape
    n = w_ada.shape[1]
    return pl.pallas_call(
        _mod_kernel,
        grid=(n // tn,),
        in_specs=[
            pl.BlockSpec((bsz, d), lambda j: (0, 0)),
            pl.BlockSpec((d, tn), lambda j: (0, j)),
            pl.BlockSpec((1, tn), lambda j: (0, j)),
        ],
        out_specs=pl.BlockSpec((bsz, tn), lambda j: (0, j)),
        out_shape=jax.ShapeDtypeStruct((bsz, n), F32),
        compiler_params=pltpu.CompilerParams(
            dimension_semantics=("arbitrary",), vmem_limit_bytes=VMEM_LIMIT_BYTES),
        name="adaln_mod",
    )(c, w_ada, b_ada.reshape(1, n))


def _prenorm_rows(x_ref, mod_ref, g_ref, h_ref, r0, n_rows):
    xv = x_ref[pl.ds(r0, n_rows), :]
    ms = jnp.mean(xv * xv, axis=-1, keepdims=True)
    y = xv * lax.rsqrt(ms + EPS) * g_ref[...]
    h_ref[pl.ds(r0, n_rows), :] = (y * (1.0 + mod_ref[0, 1:2, :]) + mod_ref[0, 0:1, :]).astype(BF16)


def _proj_kernel(x_ref, mod_ref, g_ref, wi_ref, wm_ref, b_ref, o_ref, h0_ref, h1_ref,
                 *, n_proj_tiles, sub_rows):
    i = pl.program_id(0)
    j = pl.program_id(1)
    tm, d = x_ref.shape
    rows_per_step = tm // n_proj_tiles

    def prenorm_slice(h_ref):
        for s in range(rows_per_step // sub_rows):
            r0 = pl.multiple_of(j * rows_per_step + s * sub_rows, sub_rows)
            _prenorm_rows(x_ref, mod_ref, g_ref, h_ref, r0, sub_rows)

    @pl.when(jnp.logical_and(i == 0, j < n_proj_tiles))
    def _():
        prenorm_slice(h0_ref)

    def step(h_read, h_write):
        @pl.when(j < n_proj_tiles)
        def _():
            acc = jnp.dot(h_read[...], wi_ref[...], preferred_element_type=F32)
            o_ref[...] = acc.astype(BF16)
            prenorm_slice(h_write)

        @pl.when(j >= n_proj_tiles)
        def _():
            acc = jnp.dot(h_read[...], wm_ref[...], preferred_element_type=F32)
            o_ref[...] = _sigmoid(acc + b_ref[...]).astype(BF16)

    @pl.when(jnp.logical_and(i > 0, i % 2 == 1))
    def _():
        step(h0_ref, h1_ref)

    @pl.when(jnp.logical_and(i > 0, i % 2 == 0))
    def _():
        step(h1_ref, h0_ref)


def _proj_call(x2, mod3, g_pre, w_in, w_merge, b_merge, seq, tm=1024, tn=1024):
    t, d = x2.shape
    n_proj = w_in.shape[1]
    n_merge = w_merge.shape[1]
    n_i = t // tm
    n_proj_tiles = n_proj // tn
    kern = functools.partial(_proj_kernel, n_proj_tiles=n_proj_tiles, sub_rows=32)
    cur = lambda i: jnp.minimum(i, n_i - 1)
    return pl.pallas_call(
        kern,
        grid=(n_i + 1, (n_proj + n_merge) // tn),
        in_specs=[
            pl.BlockSpec((tm, d), lambda i, j: (cur(i), 0)),
            pl.BlockSpec((1, 3, d), lambda i, j: ((cur(i) * tm) // seq, 0, 0)),
            pl.BlockSpec((1, d), lambda i, j: (0, 0)),
            pl.BlockSpec((d, tn), lambda i, j: (0, jnp.where(i > 0, jnp.minimum(j, n_proj_tiles - 1), 0))),
            pl.BlockSpec((d, tn), lambda i, j: (0, jnp.where(i > 0, jnp.maximum(j - n_proj_tiles, 0), 0))),
            pl.BlockSpec((1, tn), lambda i, j: (0, jnp.maximum(j - n_proj_tiles, 0))),
        ],
        out_specs=pl.BlockSpec((tm, tn), lambda i, j: (jnp.maximum(i - 1, 0), jnp.where(i > 0, j, 0))),
        out_shape=jax.ShapeDtypeStruct((t, n_proj + n_merge), BF16),
        scratch_shapes=[pltpu.VMEM((tm, d), BF16), pltpu.VMEM((tm, d), BF16)],
        compiler_params=pltpu.CompilerParams(
            dimension_semantics=("arbitrary", "arbitrary"), vmem_limit_bytes=VMEM_LIMIT_BYTES),
        name="prenorm_proj",
    )(x2, mod3, g_pre, w_in, w_merge, b_merge)


def _lower_bound(lbl_ref, layer):
    lbl = lbl_ref[...]
    lmax = jnp.max(lbl, axis=0, keepdims=True)
    le = jnp.exp(lbl - lmax)
    return (jnp.sum(le[0:layer + 1, :], axis=0, keepdims=True)
            / jnp.sum(le, axis=0, keepdims=True))


def _causal_mask():
    row = lax.broadcasted_iota(jnp.int32, (CHUNK, CHUNK), 0)
    col = lax.broadcasted_iota(jnp.int32, (CHUNK, CHUNK), 1)
    return col <= row


def _mix_unit(c, h, p_ref, lb, gh_ref, cw_ref, cb_ref, y_ref, state_ref, u_ref, *, d_a, d_b):
    r0 = c * CHUNK
    rows = slice(r0, r0 + CHUNK)
    dk = d_a // A_HEADS
    hs = slice(h * dk, (h + 1) * dk)
    causal = _causal_mask()
    tril = causal.astype(BF16)

    f_a = p_ref[rows, d_a + h * dk:d_a + (h + 1) * dk].astype(F32)
    lbh = lb[:, hs]
    f = lbh + (1.0 - lbh) * _sigmoid(f_a)
    log_f = jnp.log(f)
    k = 1.0 - f
    lf_hi = log_f.astype(BF16)
    lf_lo = (log_f - lf_hi.astype(F32)).astype(BF16)
    b2 = jnp.dot(tril, jnp.concatenate([lf_hi, lf_lo], axis=1), preferred_element_type=F32)

    base = 4 * d_a
    pad = CONV_PAD
    gate_cc = p_ref[rows, base + d_b + h * dk:base + d_b + (h + 1) * dk].astype(F32)
    v_b = p_ref[rows, base + 2 * d_b + h * dk:base + 2 * d_b + (h + 1) * dk].astype(F32)
    u_ref[pad + r0:pad + r0 + CHUNK, hs] = gate_cc * v_b
    conv = cb_ref[:, hs] + cw_ref[CONV_W - 1:CONV_W, hs] * u_ref[pad + r0:pad + r0 + CHUNK, hs]
    for jj in range(CONV_W - 1):
        back = CONV_W - 1 - jj
        conv = conv + cw_ref[jj:jj + 1, hs] * u_ref[pad + r0 - back:pad + r0 - back + CHUNK, hs]
    gate_bb = p_ref[rows, base + h * dk:base + (h + 1) * dk].astype(F32)
    z_b = p_ref[rows, base + 3 * d_b + h * dk:base + 3 * d_b + (h + 1) * dk].astype(F32)
    y_ref[rows, d_a + h * dk:d_a + (h + 1) * dk] = (
        gate_bb * conv * (z_b * _sigmoid(z_b))).astype(BF16)
    yield

    b = b2[:, 0:dk] + b2[:, dk:2 * dk]
    b_mid = b[CHUNK // 2 - 1:CHUNK // 2, :]
    b_last = b[CHUNK - 1:CHUNK, :]
    q = p_ref[rows, h * dk:(h + 1) * dk].astype(F32)
    q_rel = q * jnp.exp(b - b_mid)
    k_rel = k * jnp.exp(b_mid - b)
    q_abs = (q_rel * jnp.exp(b_mid)).astype(BF16)
    k_last = (k_rel * jnp.exp(b_last - b_mid)).astype(BF16)
    decay = jnp.exp(b_last)
    v = p_ref[rows, 2 * d_a + h * dk:2 * d_a + (h + 1) * dk]
    scores = lax.dot_general(q_rel.astype(BF16), k_rel.astype(BF16), (((1,), (1,)), ((), ())),
                             preferred_element_type=F32)
    d_st = lax.dot_general(v, k_last, (((0,), (0,)), ((), ())),
                           preferred_element_type=F32)
    yield

    scores = jnp.where(causal, scores, 0.0).astype(BF16)
    st = state_ref[h]
    o = jnp.dot(scores, v, preferred_element_type=F32)
    o = o + lax.dot_general(q_abs, st.astype(BF16), (((1,), (1,)), ((), ())),
                            preferred_element_type=F32)
    state_ref[h] = st * decay + d_st
    yield

    o = o * lax.rsqrt(jnp.mean(o * o, axis=-1, keepdims=True) + EPS) * gh_ref[...]
    z = p_ref[rows, 3 * d_a + h * dk:3 * d_a + (h + 1) * dk].astype(F32)
    y_ref[rows, hs] = (o * (z * _sigmoid(z))).astype(BF16)


MIX_UNIT_PHASES = 4


def _merge_piece(j, nc, y_ref, m_ref, wa_ref, wb_ref, mix_ref, *, d_a, d):
    cols = slice(j * nc, (j + 1) * nc)
    p_a = jnp.dot(y_ref[:, 0:d_a], wa_ref[:, cols], preferred_element_type=F32)
    p_b = jnp.dot(y_ref[:, d_a:], wb_ref[:, cols], preferred_element_type=F32)
    m_a = m_ref[:, j * nc:(j + 1) * nc].astype(F32)
    m_b = m_ref[:, d + j * nc:d + (j + 1) * nc].astype(F32)
    mix_ref[:, cols] = (m_a * p_a + m_b * p_b).astype(BF16)


def _wo_piece(j, nc, mix_ref, wo_ref, acc_ref):
    cols = slice(j * nc, (j + 1) * nc)
    acc_ref[:, cols] = jnp.dot(mix_ref[...], wo_ref[:, cols], preferred_element_type=F32)


def _post_piece(acc_ref, x_ref, mod_ref, g_ref, o_ref):
    out = acc_ref[...]
    normed = out * lax.rsqrt(jnp.mean(out * out, axis=-1, keepdims=True) + EPS) * g_ref[...]
    o_ref[...] = x_ref[...] + mod_ref[0, 2:3, :] * normed


def _mix_out_kernel(p_ref, lbl_ref, gh_ref, cw_ref, cb_ref, m_ref, x_ref, mod_ref, wa_ref, wb_ref,
                    wo_ref, g_ref, o_ref, state_ref, u_ref, y0_ref, y1_ref, mix_ref, acc_ref,
                    *, d_a, d_b, layer, n_blocks, blocks_per_seq, out_cols):
    k = pl.program_id(0)
    blk = jnp.minimum(k, n_blocks - 1)
    ts, d = x_ref.shape
    n_chunks = ts // CHUNK
    n_out = d // out_cols

    @pl.when(k == 0)
    def _():
        y1_ref[...] = jnp.zeros_like(y1_ref)

    @pl.when(blk % blocks_per_seq == 0)
    def _():
        state_ref[...] = jnp.zeros_like(state_ref)
        u_ref[0:CONV_PAD, :] = jnp.zeros((CONV_PAD, d_b), F32)

    def step(y_write, y_read):
        lb = _lower_bound(lbl_ref, layer)
        out_pieces = (
            [functools.partial(_merge_piece, j, out_cols, y_read, m_ref, wa_ref, wb_ref, mix_ref,
                               d_a=d_a, d=d) for j in range(n_out)]
            + [functools.partial(_wo_piece, j, out_cols, mix_ref, wo_ref, acc_ref)
               for j in range(n_out)])
        mix_units = [[_mix_unit(c, h, p_ref, lb, gh_ref, cw_ref, cb_ref, y_write, state_ref, u_ref,
                                d_a=d_a, d_b=d_b) for h in range(A_HEADS)]
                     for c in range(n_chunks)]
        groups = [t - ph for t in range(n_chunks + MIX_UNIT_PHASES - 1)
                  for ph in reversed(range(MIX_UNIT_PHASES)) if 0 <= t - ph < n_chunks]
        for gi, c in enumerate(groups):
            lo = (gi * len(out_pieces)) // len(groups)
            hi = ((gi + 1) * len(out_pieces)) // len(groups)
            for piece in out_pieces[lo:hi]:
                piece()
            for unit in mix_units[c]:
                next(unit, None)
        u_ref[0:CONV_PAD, :] = u_ref[ts:ts + CONV_PAD, :]
        _post_piece(acc_ref, x_ref, mod_ref, g_ref, o_ref)

    @pl.when(k % 2 == 0)
    def _():
        step(y0_ref, y1_ref)

    @pl.when(k % 2 == 1)
    def _():
        step(y1_ref, y0_ref)


def _mix_out_call(pcat, lb_logits, g_head, conv_w, conv_b, x2, mod3, w_up_a, w_up_b, w_o, g_post,
                  seq, layer, ts=256):
    t, d = x2.shape
    d_a = w_up_a.shape[0]
    d_b = w_up_b.shape[0]
    n_in = 4 * d_a + 4 * d_b
    dk = d_a // A_HEADS
    n_blocks = t // ts
    m_col_block = n_in // (2 * d)
    kern = functools.partial(_mix_out_kernel, d_a=d_a, d_b=d_b, layer=layer, n_blocks=n_blocks,
                             blocks_per_seq=seq // ts, out_cols=512)
    const = lambda k: (0, 0)
    cur = lambda k: jnp.minimum(k, n_blocks - 1)
    prev = lambda k: jnp.maximum(k - 1, 0)
    return pl.pallas_call(
        kern,
        grid=(n_blocks + 1,),
        in_specs=[
            pl.BlockSpec((ts, n_in), lambda k: (cur(k), 0)),
            pl.BlockSpec(lb_logits.shape, const),
            pl.BlockSpec((1, dk), const),
            pl.BlockSpec((CONV_W, d_b), const),
            pl.BlockSpec((1, d_b), const),
            pl.BlockSpec((ts, 2 * d), lambda k: (prev(k), m_col_block)),
            pl.BlockSpec((ts, d), lambda k: (prev(k), 0)),
            pl.BlockSpec((1, 3, d), lambda k: ((prev(k) * ts) // seq, 0, 0)),
            pl.BlockSpec((d_a, d), const, pipeline_mode=pl.Buffered(1)),
            pl.BlockSpec((d_b, d), const, pipeline_mode=pl.Buffered(1)),
            pl.BlockSpec((d, d), const, pipeline_mode=pl.Buffered(1)),
            pl.BlockSpec((1, d), const),
        ],
        out_specs=pl.BlockSpec((ts, d), lambda k: (prev(k), 0)),
        out_shape=jax.ShapeDtypeStruct((t, d), F32),
        scratch_shapes=[
            pltpu.VMEM((A_HEADS, dk, dk), F32),
            pltpu.VMEM((ts + CONV_PAD, d_b), F32),
            pltpu.VMEM((ts, d_a + d_b), BF16),
            pltpu.VMEM((ts, d_a + d_b), BF16),
            pltpu.VMEM((ts, d), BF16),
            pltpu.VMEM((ts, d), F32),
        ],
        compiler_params=pltpu.CompilerParams(
            dimension_semantics=("arbitrary",), vmem_limit_bytes=VMEM_LIMIT_BYTES),
        name="mix_merge_out",
    )(pcat, lb_logits, g_head, conv_w, conv_b, pcat, x2, mod3, w_up_a, w_up_b, w_o, g_post)


def kernel(x, c, w_ada, b_ada, g_pre, w_in, lb_logits, g_head_a, conv_w, conv_b,
           w_up_a, w_up_b, w_merge, b_merge, w_o, g_post):
    bsz, seq, d = x.shape
    depth = w_in.shape[0]
    d_a = w_up_a.shape[1]
    d_b = w_up_b.shape[1]
    n_proj = w_in.shape[2]
    assert n_proj == 4 * d_a + 4 * d_b and w_merge.shape[2] == 2 * d
    assert n_proj % (2 * d) == 0 and seq % CHUNK == 0
    x2 = x.reshape(bsz * seq, d)
    for l in range(depth):
        mod3 = _mod_call(c, w_ada[l], b_ada[l]).reshape(bsz, 3, d)
        pcat = _proj_call(x2, mod3, g_pre[l:l + 1], w_in[l].astype(BF16), w_merge[l].astype(BF16),
                          b_merge[l:l + 1], seq)
        x2 = _mix_out_call(pcat, lb_logits, g_head_a[l:l + 1], conv_w[l], conv_b[l:l + 1], x2, mod3,
                           w_up_a[l].astype(BF16), w_up_b[l].astype(BF16), w_o[l].astype(BF16),
                           g_post[l:l + 1], seq, l)
    return x2.reshape(bsz, seq, d)
```

```python
import functools

import jax
import jax.numpy as jnp
from jax import lax
from jax.experimental import pallas as pl
from jax.experimental.pallas import tpu as pltpu

F32 = jnp.float32
BF16 = jnp.bfloat16

EPS = 1e-6
CHUNK = 64
A_HEADS = 8
CONV_W = 3
CONV_PAD = 8

VMEM_LIMIT_BYTES = 56 * 1024 * 1024


def _sigmoid(v):
    return 1.0 / (1.0 + jnp.exp(-v))


def _mod_kernel(c_ref, w_ref, b_ref, o_ref):
    c = c_ref[...]
    c_act = c * _sigmoid(c)
    acc = jnp.dot(c_act.astype(BF16), w_ref[...].astype(BF16), preferred_element_type=F32)
    o_ref[...] = acc + b_ref[...]


def _mod_call(c, w_ada, b_ada, tn=1024):
    bsz, d = c.shape
    n = w_ada.shape[1]
    return pl.pallas_call(
        _mod_kernel,
        grid=(n // tn,),
        in_specs=[
            pl.BlockSpec((bsz, d), lambda j: (0, 0)),
            pl.BlockSpec((d, tn), lambda j: (0, j)),
            pl.BlockSpec((1, tn), lambda j: (0, j)),
        ],
        out_specs=pl.BlockSpec((bsz, tn), lambda j: (0, j)),
        out_shape=jax.ShapeDtypeStruct((bsz, n), F32),
        compiler_params=pltpu.CompilerParams(
            dimension_semantics=("arbitrary",), vmem_limit_bytes=VMEM_LIMIT_BYTES),
        name="adaln_mod",
    )(c, w_ada, b_ada.reshape(1, n))


def _prenorm_rows(x_ref, mod_ref, g_ref, h_ref, r0, n_rows):
    xv = x_ref[pl.ds(r0, n_rows), :]
    ms = jnp.mean(xv * xv, axis=-1, keepdims=True)
    y = xv * lax.rsqrt(ms + EPS) * g_ref[...]
    h_ref[pl.ds(r0, n_rows), :] = (y * (1.0 + mod_ref[0, 1:2, :]) + mod_ref[0, 0:1, :]).astype(BF16)


def _proj_kernel(x_ref, mod_ref, g_ref, wi_ref, wm_ref, b_ref, wa32_ref, wb32_ref, wo32_ref,
                 o_ref, wa16_ref, wb16_ref, wo16_ref, h0_ref, h1_ref,
                 *, n_proj_tiles, sub_rows, n_cast_steps):
    i = pl.program_id(0)
    j = pl.program_id(1)
    tm, d = x_ref.shape
    rows_per_step = tm // n_proj_tiles

    @pl.when(i * pl.num_programs(1) + j < n_cast_steps)
    def _():
        wa16_ref[...] = wa32_ref[...].astype(BF16)
        wb16_ref[...] = wb32_ref[...].astype(BF16)
        wo16_ref[...] = wo32_ref[...].astype(BF16)

    def prenorm_slice(h_ref):
        for s in range(rows_per_step // sub_rows):
            r0 = pl.multiple_of(j * rows_per_step + s * sub_rows, sub_rows)
            _prenorm_rows(x_ref, mod_ref, g_ref, h_ref, r0, sub_rows)

    @pl.when(jnp.logical_and(i == 0, j < n_proj_tiles))
    def _():
        prenorm_slice(h0_ref)

    def step(h_read, h_write):
        @pl.when(j < n_proj_tiles)
        def _():
            acc = jnp.dot(h_read[...], wi_ref[...], preferred_element_type=F32)
            o_ref[...] = acc.astype(BF16)
            prenorm_slice(h_write)

        @pl.when(j >= n_proj_tiles)
        def _():
            acc = jnp.dot(h_read[...], wm_ref[...], preferred_element_type=F32)
            o_ref[...] = _sigmoid(acc + b_ref[...]).astype(BF16)

    @pl.when(jnp.logical_and(i > 0, i % 2 == 1))
    def _():
        step(h0_ref, h1_ref)

    @pl.when(jnp.logical_and(i > 0, i % 2 == 0))
    def _():
        step(h1_ref, h0_ref)


def _proj_call(x2, mod3, g_pre, w_in, w_merge, b_merge, w_up_a, w_up_b, w_o, seq,
               tm=1024, tn=1024, cast_rows=64):
    t, d = x2.shape
    n_proj = w_in.shape[1]
    n_merge = w_merge.shape[1]
    n_i = t // tm
    n_j = (n_proj + n_merge) // tn
    n_proj_tiles = n_proj // tn
    cur = lambda i: jnp.minimum(i, n_i - 1)

    cast_blocks = [w.shape[0] // cast_rows for w in (w_up_a, w_up_b, w_o)]
    assert sum(cast_blocks) <= (n_i + 1) * n_j
    cast_starts = [0, cast_blocks[0], cast_blocks[0] + cast_blocks[1]]
    kern = functools.partial(_proj_kernel, n_proj_tiles=n_proj_tiles, sub_rows=32,
                             n_cast_steps=sum(cast_blocks))

    def cast_spec(w, start, n_blk):
        return pl.BlockSpec((cast_rows, w.shape[1]),
                            lambda i, j: (jnp.clip(i * n_j + j - start, 0, n_blk - 1), 0))

    cast_specs = [cast_spec(w, s, n) for w, s, n in zip((w_up_a, w_up_b, w_o), cast_starts, cast_blocks)]
    return pl.pallas_call(
        kern,
        grid=(n_i + 1, n_j),
        in_specs=[
            pl.BlockSpec((tm, d), lambda i, j: (cur(i), 0)),
            pl.BlockSpec((1, 3, d), lambda i, j: ((cur(i) * tm) // seq, 0, 0)),
            pl.BlockSpec((1, d), lambda i, j: (0, 0)),
            pl.BlockSpec((d, tn), lambda i, j: (0, jnp.where(i > 0, jnp.minimum(j, n_proj_tiles - 1), 0))),
            pl.BlockSpec((d, tn), lambda i, j: (0, jnp.where(i > 0, jnp.maximum(j - n_proj_tiles, 0), 0))),
            pl.BlockSpec((1, tn), lambda i, j: (0, jnp.maximum(j - n_proj_tiles, 0))),
        ] + cast_specs,
        out_specs=[pl.BlockSpec((tm, tn), lambda i, j: (jnp.maximum(i - 1, 0), jnp.where(i > 0, j, 0)))]
        + cast_specs,
        out_shape=[jax.ShapeDtypeStruct((t, n_proj + n_merge), BF16)]
        + [jax.ShapeDtypeStruct(w.shape, BF16) for w in (w_up_a, w_up_b, w_o)],
        scratch_shapes=[pltpu.VMEM((tm, d), BF16), pltpu.VMEM((tm, d), BF16)],
        compiler_params=pltpu.CompilerParams(
            dimension_semantics=("arbitrary", "arbitrary"), vmem_limit_bytes=VMEM_LIMIT_BYTES),
        name="prenorm_proj",
    )(x2, mod3, g_pre, w_in, w_merge, b_merge, w_up_a, w_up_b, w_o)


def _lower_bound(lbl_ref, layer):
    lbl = lbl_ref[...]
    lmax = jnp.max(lbl, axis=0, keepdims=True)
    le = jnp.exp(lbl - lmax)
    return (jnp.sum(le[0:layer + 1, :], axis=0, keepdims=True)
            / jnp.sum(le, axis=0, keepdims=True))


def _causal_mask():
    row = lax.broadcasted_iota(jnp.int32, (CHUNK, CHUNK), 0)
    col = lax.broadcasted_iota(jnp.int32, (CHUNK, CHUNK), 1)
    return col <= row


def _mix_chunk_decay(c, p_ref, lb, *, d_a):
    rows = slice(c * CHUNK, (c + 1) * CHUNK)
    tril = _causal_mask().astype(BF16)
    f_a = p_ref[rows, d_a:2 * d_a].astype(F32)
    f = lb + (1.0 - lb) * _sigmoid(f_a)
    log_f = jnp.log(f)
    lf_hi = log_f.astype(BF16)
    lf_lo = (log_f - lf_hi.astype(F32)).astype(BF16)
    b2 = jnp.dot(tril, jnp.concatenate([lf_hi, lf_lo], axis=1), preferred_element_type=F32)
    return 1.0 - f, b2


def _mix_unit(c, h, decay_of_chunk, p_ref, gh_ref, cw_ref, cb_ref, y_ref, state_ref, u_ref, *, d_a, d_b):
    r0 = c * CHUNK
    rows = slice(r0, r0 + CHUNK)
    dk = d_a // A_HEADS
    hs = slice(h * dk, (h + 1) * dk)
    causal = _causal_mask()
    k_all, b2_all = decay_of_chunk[c]
    k = k_all[:, hs]

    b = b2_all[:, hs] + b2_all[:, d_a + h * dk:d_a + (h + 1) * dk]
    b_mid = b[CHUNK // 2 - 1:CHUNK // 2, :]
    b_last = b[CHUNK - 1:CHUNK, :]
    q = p_ref[rows, h * dk:(h + 1) * dk].astype(F32)
    q_rel = q * jnp.exp(b - b_mid)
    k_rel = k * jnp.exp(b_mid - b)
    q_abs = (q_rel * jnp.exp(b_mid)).astype(BF16)
    k_last = (k_rel * jnp.exp(b_last - b_mid)).astype(BF16)
    decay = jnp.exp(b_last)
    v = p_ref[rows, 2 * d_a + h * dk:2 * d_a + (h + 1) * dk]
    scores = lax.dot_general(q_rel.astype(BF16), k_rel.astype(BF16), (((1,), (1,)), ((), ())),
                             preferred_element_type=F32)
    d_st = lax.dot_general(v, k_last, (((0,), (0,)), ((), ())),
                           preferred_element_type=F32)
    yield

    scores = jnp.where(causal, scores, 0.0).astype(BF16)
    st = state_ref[h]
    o = jnp.dot(scores, v, preferred_element_type=F32)
    o = o + lax.dot_general(q_abs, st.astype(BF16), (((1,), (1,)), ((), ())),
                            preferred_element_type=F32)
    state_ref[h] = st * decay + d_st
    yield

    o = o * lax.rsqrt(jnp.mean(o * o, axis=-1, keepdims=True) + EPS) * gh_ref[...]
    z = p_ref[rows, 3 * d_a + h * dk:3 * d_a + (h + 1) * dk].astype(F32)
    y_ref[rows, hs] = (o * (z * _sigmoid(z))).astype(BF16)

    base = 4 * d_a
    pad = CONV_PAD
    gate_cc = p_ref[rows, base + d_b + h * dk:base + d_b + (h + 1) * dk].astype(F32)
    v_b = p_ref[rows, base + 2 * d_b + h * dk:base + 2 * d_b + (h + 1) * dk].astype(F32)
    u_ref[pad + r0:pad + r0 + CHUNK, hs] = gate_cc * v_b
    conv = cb_ref[:, hs] + cw_ref[CONV_W - 1:CONV_W, hs] * u_ref[pad + r0:pad + r0 + CHUNK, hs]
    for jj in range(CONV_W - 1):
        back = CONV_W - 1 - jj
        conv = conv + cw_ref[jj:jj + 1, hs] * u_ref[pad + r0 - back:pad + r0 - back + CHUNK, hs]
    gate_bb = p_ref[rows, base + h * dk:base + (h + 1) * dk].astype(F32)
    z_b = p_ref[rows, base + 3 * d_b + h * dk:base + 3 * d_b + (h + 1) * dk].astype(F32)
    y_ref[rows, d_a + h * dk:d_a + (h + 1) * dk] = (
        gate_bb * conv * (z_b * _sigmoid(z_b))).astype(BF16)


MIX_UNIT_PHASES = 4


def _merge_piece(j, nc, y_ref, m_ref, wa_ref, wb_ref, mix_ref, *, d_a, d):
    cols = slice(j * nc, (j + 1) * nc)
    p_a = jnp.dot(y_ref[:, 0:d_a], wa_ref[:, cols], preferred_element_type=F32)
    p_b = jnp.dot(y_ref[:, d_a:], wb_ref[:, cols], preferred_element_type=F32)
    m_a = m_ref[:, j * nc:(j + 1) * nc].astype(F32)
    m_b = m_ref[:, d + j * nc:d + (j + 1) * nc].astype(F32)
    mix_ref[:, cols] = (m_a * p_a + m_b * p_b).astype(BF16)


def _wo_piece(j, nc, mix_ref, wo_ref, acc_ref):
    cols = slice(j * nc, (j + 1) * nc)
    acc_ref[:, cols] = jnp.dot(mix_ref[...], wo_ref[:, cols], preferred_element_type=F32)


def _post_piece(acc_ref, x_ref, mod_ref, g_ref, o_ref):
    out = acc_ref[...]
    normed = out * lax.rsqrt(jnp.mean(out * out, axis=-1, keepdims=True) + EPS) * g_ref[...]
    o_ref[...] = x_ref[...] + mod_ref[0, 2:3, :] * normed


def _mix_out_kernel(p_ref, lbl_ref, gh_ref, cw_ref, cb_ref, m_ref, x_ref, mod_ref, wa_ref, wb_ref,
                    wo_ref, g_ref, o_ref, state_ref, u_ref, y0_ref, y1_ref, mix_ref, acc_ref,
                    *, d_a, d_b, layer, n_blocks, blocks_per_seq, out_cols):
    k = pl.program_id(0)
    blk = jnp.minimum(k, n_blocks - 1)
    ts, d = x_ref.shape
    n_chunks = ts // CHUNK
    n_out = d // out_cols

    @pl.when(k == 0)
    def _():
        y1_ref[...] = jnp.zeros_like(y1_ref)

    @pl.when(blk % blocks_per_seq == 0)
    def _():
        state_ref[...] = jnp.zeros_like(state_ref)
        u_ref[0:CONV_PAD, :] = jnp.zeros((CONV_PAD, d_b), F32)

    def step(y_write, y_read):
        lb = _lower_bound(lbl_ref, layer)
        out_pieces = (
            [functools.partial(_merge_piece, j, out_cols, y_read, m_ref, wa_ref, wb_ref, mix_ref,
                               d_a=d_a, d=d) for j in range(n_out)]
            + [functools.partial(_wo_piece, j, out_cols, mix_ref, wo_ref, acc_ref)
               for j in range(n_out)])
        decay_of_chunk = {}
        mix_units = [[_mix_unit(c, h, decay_of_chunk, p_ref, gh_ref, cw_ref, cb_ref, y_write, state_ref,
                                u_ref, d_a=d_a, d_b=d_b) for h in range(A_HEADS)]
                     for c in range(n_chunks)]
        groups = [(t - ph, ph) for t in range(n_chunks + MIX_UNIT_PHASES - 1)
                  for ph in reversed(range(MIX_UNIT_PHASES)) if 0 <= t - ph < n_chunks]
        for gi, (c, ph) in enumerate(groups):
            lo = -((-gi * len(out_pieces)) // len(groups))
            hi = -((-(gi + 1) * len(out_pieces)) // len(groups))
            for piece in out_pieces[lo:hi]:
                piece()
            if ph == 0:
                decay_of_chunk[c] = _mix_chunk_decay(c, p_ref, lb, d_a=d_a)
            else:
                for unit in mix_units[c]:
                    next(unit, None)
        u_ref[0:CONV_PAD, :] = u_ref[ts:ts + CONV_PAD, :]
        _post_piece(acc_ref, x_ref, mod_ref, g_ref, o_ref)

    @pl.when(k % 2 == 0)
    def _():
        step(y0_ref, y1_ref)

    @pl.when(k % 2 == 1)
    def _():
        step(y1_ref, y0_ref)


def _mix_out_call(pcat, lb_logits, g_head, conv_w, conv_b, x2, mod3, w_up_a, w_up_b, w_o, g_post,
                  seq, layer, ts=256):
    t, d = x2.shape
    d_a = w_up_a.shape[0]
    d_b = w_up_b.shape[0]
    n_in = 4 * d_a + 4 * d_b
    dk = d_a // A_HEADS
    n_blocks = t // ts
    m_col_block = n_in // (2 * d)
    kern = functools.partial(_mix_out_kernel, d_a=d_a, d_b=d_b, layer=layer, n_blocks=n_blocks,
                             blocks_per_seq=seq // ts, out_cols=512)
    const = lambda k: (0, 0)
    cur = lambda k: jnp.minimum(k, n_blocks - 1)
    prev = lambda k: jnp.maximum(k - 1, 0)
    return pl.pallas_call(
        kern,
        grid=(n_blocks + 1,),
        in_specs=[
            pl.BlockSpec((ts, n_in), lambda k: (cur(k), 0)),
            pl.BlockSpec(lb_logits.shape, const),
            pl.BlockSpec((1, dk), const),
            pl.BlockSpec((CONV_W, d_b), const),
            pl.BlockSpec((1, d_b), const),
            pl.BlockSpec((ts, 2 * d), lambda k: (prev(k), m_col_block)),
            pl.BlockSpec((ts, d), lambda k: (prev(k), 0)),
            pl.BlockSpec((1, 3, d), lambda k: ((prev(k) * ts) // seq, 0, 0)),
            pl.BlockSpec((d_a, d), const, pipeline_mode=pl.Buffered(1)),
            pl.BlockSpec((d_b, d), const, pipeline_mode=pl.Buffered(1)),
            pl.BlockSpec((d, d), const, pipeline_mode=pl.Buffered(1)),
            pl.BlockSpec((1, d), const),
        ],
        out_specs=pl.BlockSpec((ts, d), lambda k: (prev(k), 0)),
        out_shape=jax.ShapeDtypeStruct((t, d), F32),
        scratch_shapes=[
            pltpu.VMEM((A_HEADS, dk, dk), F32),
            pltpu.VMEM((ts + CONV_PAD, d_b), F32),
            pltpu.VMEM((ts, d_a + d_b), BF16),
            pltpu.VMEM((ts, d_a + d_b), BF16),
            pltpu.VMEM((ts, d), BF16),
            pltpu.VMEM((ts, d), F32),
        ],
        compiler_params=pltpu.CompilerParams(
            dimension_semantics=("arbitrary",), vmem_limit_bytes=VMEM_LIMIT_BYTES),
        name="mix_merge_out",
    )(pcat, lb_logits, g_head, conv_w, conv_b, pcat, x2, mod3, w_up_a, w_up_b, w_o, g_post)


def kernel(x, c, w_ada, b_ada, g_pre, w_in, lb_logits, g_head_a, conv_w, conv_b,
           w_up_a, w_up_b, w_merge, b_merge, w_o, g_post):
    bsz, seq, d = x.shape
    depth = w_in.shape[0]
    d_a = w_up_a.shape[1]
    d_b = w_up_b.shape[1]
    n_proj = w_in.shape[2]
    assert n_proj == 4 * d_a + 4 * d_b and w_merge.shape[2] == 2 * d
    assert n_proj % (2 * d) == 0 and seq % CHUNK == 0
    x2 = x.reshape(bsz * seq, d)
    for l in range(depth):
        mod3 = _mod_call(c, w_ada[l], b_ada[l]).reshape(bsz, 3, d)
        pcat, wa16, wb16, wo16 = _proj_call(
            x2, mod3, g_pre[l:l + 1], w_in[l].astype(BF16), w_merge[l].astype(BF16), b_merge[l:l + 1],
            w_up_a[l], w_up_b[l], w_o[l], seq)
        x2 = _mix_out_call(pcat, lb_logits, g_head_a[l:l + 1], conv_w[l], conv_b[l:l + 1], x2, mod3,
                           wa16, wb16, wo16, g_post[l:l + 1], seq, l)
    return x2.reshape(bsz, seq, d)
```

```python
import functools

import jax
import jax.numpy as jnp
from jax import lax
from jax.experimental import pallas as pl
from jax.experimental.pallas import tpu as pltpu

F32 = jnp.float32
BF16 = jnp.bfloat16

EPS = 1e-6
CHUNK = 64
A_HEADS = 8
CONV_W = 3
CONV_PAD = 8

VMEM_LIMIT_BYTES = 56 * 1024 * 1024


def _sigmoid(v):
    return 1.0 / (1.0 + jnp.exp(-v))


def _mod_kernel(c_ref, w_ref, b_ref, o_ref):
    c = c_ref[...]
    c_act = c * _sigmoid(c)
    acc = jnp.dot(c_act.astype(BF16), w_ref[...].astype(BF16), preferred_element_type=F32)
    o_ref[...] = acc + b_ref[...]


def _mod_call(c, w_ada, b_ada, tn=1024):
    bsz, d = c.shape
    n = w_ada.shape[1]
    return pl.pallas_call(
        _mod_kernel,
        grid=(n // tn,),
        in_specs=[
            pl.BlockSpec((bsz, d), lambda j: (0, 0)),
            pl.BlockSpec((d, tn), lambda j: (0, j)),
            pl.BlockSpec((1, tn), lambda j: (0, j)),
        ],
        out_specs=pl.BlockSpec((bsz, tn), lambda j: (0, j)),
        out_shape=jax.ShapeDtypeStruct((bsz, n), F32),
        compiler_params=pltpu.CompilerParams(
            dimension_semantics=("arbitrary",), vmem_limit_bytes=VMEM_LIMIT_BYTES),
        name="adaln_mod",
    )(c, w_ada, b_ada.reshape(1, n))


def _silu(v):
    return v * _sigmoid(v)


def _proj_kernel(x_ref, mod_ref, g_ref, wi_ref, wm_ref, b_ref, cw_ref, cb_ref, o_ref,
                 h_ref, gbb_ref, u_ref, *, n_proj_tiles, sub_rows, blocks_per_seq):
    i = pl.program_id(0)
    j = pl.program_id(1)
    tm, d = x_ref.shape
    rows_per_step = tm // n_proj_tiles
    pad = CONV_PAD
    cur = (i + 1) % 2
    nxt = i % 2

    def prenorm_slice(half):
        for s in range(rows_per_step // sub_rows):
            r0 = pl.multiple_of(j * rows_per_step + s * sub_rows, sub_rows)
            xv = x_ref[pl.ds(r0, sub_rows), :]
            ms = jnp.mean(xv * xv, axis=-1, keepdims=True)
            y = xv * lax.rsqrt(ms + EPS) * g_ref[...]
            h_ref[half, pl.ds(r0, sub_rows), :] = (
                y * (1.0 + mod_ref[0, 1:2, :]) + mod_ref[0, 0:1, :]).astype(BF16)

    @pl.when(jnp.logical_and(i == 0, j < n_proj_tiles))
    def _():
        prenorm_slice(0)

    def proj_tile(epilogue, before_dot=None):
        if before_dot is not None:
            before_dot()
        acc = jnp.dot(h_ref[cur], wi_ref[...], preferred_element_type=F32)
        epilogue(acc)
        prenorm_slice(nxt)

    def store_plain(acc):
        o_ref[...] = acc.astype(BF16)

    def store_silu(acc):
        o_ref[...] = _silu(acc).astype(BF16)

    def keep_gate_bb(acc):
        gbb_ref[...] = acc.astype(BF16)

    def keep_gate_cc(acc):
        u_ref[pad:pad + tm, :] = acc

    def times_v_b(acc):
        u_ref[pad:pad + tm, :] = u_ref[pad:pad + tm, :] * acc

    def conv_times_gate_bb():
        conv = cb_ref[...] + cw_ref[CONV_W - 1:CONV_W, :] * u_ref[pad:pad + tm, :]
        for jj in range(CONV_W - 1):
            back = CONV_W - 1 - jj
            conv = conv + cw_ref[jj:jj + 1, :] * u_ref[pad - back:pad - back + tm, :]
        gbb_ref[...] = (gbb_ref[...].astype(F32) * conv).astype(BF16)
        u_ref[0:pad, :] = u_ref[tm:tm + pad, :]

    def gate_z_b_and_store(acc):
        o_ref[...] = (gbb_ref[...].astype(F32) * _silu(acc)).astype(BF16)

    epilogues = [store_plain, store_plain, store_plain, store_silu,
                 keep_gate_bb, keep_gate_cc, times_v_b, gate_z_b_and_store]
    prologues = [None] * (n_proj_tiles - 1) + [conv_times_gate_bb]
    assert len(epilogues) == n_proj_tiles

    @pl.when(jnp.logical_and(i > 0, jnp.logical_and(j == 5, (i - 1) % blocks_per_seq == 0)))
    def _():
        u_ref[0:pad, :] = jnp.zeros((pad, u_ref.shape[1]), F32)

    for jj, (epilogue, prologue) in enumerate(zip(epilogues, prologues)):
        @pl.when(jnp.logical_and(i > 0, j == jj))
        def _(epilogue=epilogue, prologue=prologue):
            proj_tile(epilogue, prologue)

    @pl.when(jnp.logical_and(i > 0, j >= n_proj_tiles))
    def _():
        acc = jnp.dot(h_ref[cur], wm_ref[...], preferred_element_type=F32)
        o_ref[...] = _sigmoid(acc + b_ref[...]).astype(BF16)


def _proj_out_col(j, n_proj_tiles, n_kept, n_merge_tiles):
    return jnp.where(j < n_kept, j,
                     jnp.where(j < n_proj_tiles, n_kept + n_merge_tiles, j - (n_proj_tiles - n_kept)))


def _proj_call(x2, mod3, g_pre, w_in, w_merge, b_merge, conv_w, conv_b, seq, tm=1024, tn=1024):
    t, d = x2.shape
    n_proj = w_in.shape[1]
    n_merge = w_merge.shape[1]
    n_i = t // tm
    n_proj_tiles = n_proj // tn
    n_merge_tiles = n_merge // tn
    n_kept = 4
    assert n_proj_tiles == 8 and conv_w.shape[1] == tn and seq % tm == 0
    kern = functools.partial(_proj_kernel, n_proj_tiles=n_proj_tiles, sub_rows=32,
                             blocks_per_seq=seq // tm)
    cur = lambda i: jnp.minimum(i, n_i - 1)
    return pl.pallas_call(
        kern,
        grid=(n_i + 1, n_proj_tiles + n_merge_tiles),
        in_specs=[
            pl.BlockSpec((tm, d), lambda i, j: (cur(i), 0)),
            pl.BlockSpec((1, 3, d), lambda i, j: ((cur(i) * tm) // seq, 0, 0)),
            pl.BlockSpec((1, d), lambda i, j: (0, 0)),
            pl.BlockSpec((d, tn), lambda i, j: (0, jnp.where(i > 0, jnp.minimum(j, n_proj_tiles - 1), 0))),
            pl.BlockSpec((d, tn), lambda i, j: (0, jnp.where(i > 0, jnp.maximum(j - n_proj_tiles, 0), 0))),
            pl.BlockSpec((1, tn), lambda i, j: (0, jnp.maximum(j - n_proj_tiles, 0))),
            pl.BlockSpec((CONV_W, tn), lambda i, j: (0, 0)),
            pl.BlockSpec((1, tn), lambda i, j: (0, 0)),
        ],
        out_specs=pl.BlockSpec(
            (tm, tn), lambda i, j: (jnp.maximum(i - 1, 0),
                                    jnp.where(i > 0, _proj_out_col(j, n_proj_tiles, n_kept, n_merge_tiles), 0))),
        out_shape=jax.ShapeDtypeStruct((t, (n_kept + n_merge_tiles + 1) * tn), BF16),
        scratch_shapes=[pltpu.VMEM((2, tm, d), BF16), pltpu.VMEM((tm, tn), BF16),
                        pltpu.VMEM((tm + CONV_PAD, tn), F32)],
        compiler_params=pltpu.CompilerParams(
            dimension_semantics=("arbitrary", "arbitrary"), vmem_limit_bytes=VMEM_LIMIT_BYTES),
        name="prenorm_proj",
    )(x2, mod3, g_pre, w_in, w_merge, b_merge, conv_w, conv_b)


def _lower_bound(lbl_ref, layer):
    lbl = lbl_ref[...]
    lmax = jnp.max(lbl, axis=0, keepdims=True)
    le = jnp.exp(lbl - lmax)
    return (jnp.sum(le[0:layer + 1, :], axis=0, keepdims=True)
            / jnp.sum(le, axis=0, keepdims=True))


def _causal_mask():
    row = lax.broadcasted_iota(jnp.int32, (CHUNK, CHUNK), 0)
    col = lax.broadcasted_iota(jnp.int32, (CHUNK, CHUNK), 1)
    return col <= row


def _mix_unit(c, h, p_ref, lb, gh_ref, y_ref, state_ref, *, d_a):
    r0 = c * CHUNK
    rows = slice(r0, r0 + CHUNK)
    dk = d_a // A_HEADS
    hs = slice(h * dk, (h + 1) * dk)
    causal = _causal_mask()
    tril = causal.astype(BF16)

    f_a = p_ref[rows, d_a + h * dk:d_a + (h + 1) * dk].astype(F32)
    lbh = lb[:, hs]
    f = lbh + (1.0 - lbh) * _sigmoid(f_a)
    log_f = jnp.log(f)
    k = 1.0 - f
    lf_hi = log_f.astype(BF16)
    lf_lo = (log_f - lf_hi.astype(F32)).astype(BF16)
    b2 = jnp.dot(tril, jnp.concatenate([lf_hi, lf_lo], axis=1), preferred_element_type=F32)
    yield

    b = b2[:, 0:dk] + b2[:, dk:2 * dk]
    b_mid = b[CHUNK // 2 - 1:CHUNK // 2, :]
    b_last = b[CHUNK - 1:CHUNK, :]
    q = p_ref[rows, h * dk:(h + 1) * dk].astype(F32)
    q_rel = q * jnp.exp(b - b_mid)
    k_rel = k * jnp.exp(b_mid - b)
    q_abs = (q_rel * jnp.exp(b_mid)).astype(BF16)
    k_last = (k_rel * jnp.exp(b_last - b_mid)).astype(BF16)
    decay = jnp.exp(b_last)
    v = p_ref[rows, 2 * d_a + h * dk:2 * d_a + (h + 1) * dk]
    scores = lax.dot_general(q_rel.astype(BF16), k_rel.astype(BF16), (((1,), (1,)), ((), ())),
                             preferred_element_type=F32)
    d_st = lax.dot_general(v, k_last, (((0,), (0,)), ((), ())),
                           preferred_element_type=F32)
    yield

    scores = jnp.where(causal, scores, 0.0).astype(BF16)
    st = state_ref[h]
    o = jnp.dot(scores, v, preferred_element_type=F32)
    o = o + lax.dot_general(q_abs, st.astype(BF16), (((1,), (1,)), ((), ())),
                            preferred_element_type=F32)
    state_ref[h] = st * decay + d_st
    yield

    o = o * lax.rsqrt(jnp.mean(o * o, axis=-1, keepdims=True) + EPS) * gh_ref[...]
    silu_z = p_ref[rows, 3 * d_a + h * dk:3 * d_a + (h + 1) * dk].astype(F32)
    y_ref[rows, hs] = (o * silu_z).astype(BF16)


MIX_UNIT_PHASES = 4


def _merge_piece(j, nc, ya_ref, yb_ref, m_ref, wa_ref, wb_ref, mix_ref, *, d):
    cols = slice(j * nc, (j + 1) * nc)
    p_a = jnp.dot(ya_ref[...], wa_ref[:, cols], preferred_element_type=F32)
    p_b = jnp.dot(yb_ref[...], wb_ref[:, cols], preferred_element_type=F32)
    m_a = m_ref[:, j * nc:(j + 1) * nc].astype(F32)
    m_b = m_ref[:, d + j * nc:d + (j + 1) * nc].astype(F32)
    mix_ref[:, cols] = (m_a * p_a + m_b * p_b).astype(BF16)


def _wo_piece(j, nc, mix_ref, wo_ref, acc_ref):
    cols = slice(j * nc, (j + 1) * nc)
    acc_ref[:, cols] = jnp.dot(mix_ref[...], wo_ref[:, cols], preferred_element_type=F32)


def _post_piece(acc_ref, x_ref, mod_ref, g_ref, o_ref):
    out = acc_ref[...]
    normed = out * lax.rsqrt(jnp.mean(out * out, axis=-1, keepdims=True) + EPS) * g_ref[...]
    o_ref[...] = x_ref[...] + mod_ref[0, 2:3, :] * normed


def _mix_out_kernel(p_ref, lbl_ref, gh_ref, yb_ref, m_ref, x_ref, mod_ref, wa_ref, wb_ref,
                    wo_ref, g_ref, o_ref, state_ref, y0_ref, y1_ref, mix_ref, acc_ref,
                    *, d_a, layer, n_blocks, blocks_per_seq, out_cols):
    k = pl.program_id(0)
    blk = jnp.minimum(k, n_blocks - 1)
    ts, d = x_ref.shape
    n_chunks = ts // CHUNK
    n_out = d // out_cols

    @pl.when(k == 0)
    def _():
        y1_ref[...] = jnp.zeros_like(y1_ref)

    @pl.when(blk % blocks_per_seq == 0)
    def _():
        state_ref[...] = jnp.zeros_like(state_ref)

    def step(y_write, y_read):
        lb = _lower_bound(lbl_ref, layer)
        out_pieces = (
            [functools.partial(_merge_piece, j, out_cols, y_read, yb_ref, m_ref, wa_ref, wb_ref, mix_ref,
                               d=d) for j in range(n_out)]
            + [functools.partial(_wo_piece, j, out_cols, mix_ref, wo_ref, acc_ref)
               for j in range(n_out)])
        mix_units = [[_mix_unit(c, h, p_ref, lb, gh_ref, y_write, state_ref, d_a=d_a)
                      for h in range(A_HEADS)] for c in range(n_chunks)]
        groups = [t - ph for t in range(n_chunks + MIX_UNIT_PHASES - 1)
                  for ph in reversed(range(MIX_UNIT_PHASES)) if 0 <= t - ph < n_chunks]
        for gi, c in enumerate(groups):
            lo = (gi * len(out_pieces)) // len(groups)
            hi = ((gi + 1) * len(out_pieces)) // len(groups)
            for piece in out_pieces[lo:hi]:
                piece()
            for unit in mix_units[c]:
                next(unit, None)
        _post_piece(acc_ref, x_ref, mod_ref, g_ref, o_ref)

    @pl.when(k % 2 == 0)
    def _():
        step(y0_ref, y1_ref)

    @pl.when(k % 2 == 1)
    def _():
        step(y1_ref, y0_ref)


def _mix_out_call(pcat, lb_logits, g_head, x2, mod3, w_up_a, w_up_b, w_o, g_post, seq, layer, ts=256):
    t, d = x2.shape
    d_a = w_up_a.shape[0]
    d_b = w_up_b.shape[0]
    dk = d_a // A_HEADS
    n_blocks = t // ts
    assert 4 * d_a == 2 * d and (4 * d_a + 2 * d) % d_b == 0
    kern = functools.partial(_mix_out_kernel, d_a=d_a, layer=layer, n_blocks=n_blocks,
                             blocks_per_seq=seq // ts, out_cols=512)
    const = lambda k: (0, 0)
    cur = lambda k: jnp.minimum(k, n_blocks - 1)
    prev = lambda k: jnp.maximum(k - 1, 0)
    return pl.pallas_call(
        kern,
        grid=(n_blocks + 1,),
        in_specs=[
            pl.BlockSpec((ts, 4 * d_a), lambda k: (cur(k), 0)),
            pl.BlockSpec(lb_logits.shape, const),
            pl.BlockSpec((1, dk), const),
            pl.BlockSpec((ts, d_b), lambda k: (prev(k), (4 * d_a + 2 * d) // d_b)),
            pl.BlockSpec((ts, 2 * d), lambda k: (prev(k), 1)),
            pl.BlockSpec((ts, d), lambda k: (prev(k), 0)),
            pl.BlockSpec((1, 3, d), lambda k: ((prev(k) * ts) // seq, 0, 0)),
            pl.BlockSpec((d_a, d), const, pipeline_mode=pl.Buffered(1)),
            pl.BlockSpec((d_b, d), const, pipeline_mode=pl.Buffered(1)),
            pl.BlockSpec((d, d), const, pipeline_mode=pl.Buffered(1)),
            pl.BlockSpec((1, d), const),
        ],
        out_specs=pl.BlockSpec((ts, d), lambda k: (prev(k), 0)),
        out_shape=jax.ShapeDtypeStruct((t, d), F32),
        scratch_shapes=[
            pltpu.VMEM((A_HEADS, dk, dk), F32),
            pltpu.VMEM((ts, d_a), BF16),
            pltpu.VMEM((ts, d_a), BF16),
            pltpu.VMEM((ts, d), BF16),
            pltpu.VMEM((ts, d), F32),
        ],
        compiler_params=pltpu.CompilerParams(
            dimension_semantics=("arbitrary",), vmem_limit_bytes=VMEM_LIMIT_BYTES),
        name="mix_merge_out",
    )(pcat, lb_logits, g_head, pcat, pcat, x2, mod3, w_up_a, w_up_b, w_o, g_post)


def kernel(x, c, w_ada, b_ada, g_pre, w_in, lb_logits, g_head_a, conv_w, conv_b,
           w_up_a, w_up_b, w_merge, b_merge, w_o, g_post):
    bsz, seq, d = x.shape
    depth = w_in.shape[0]
    d_a = w_up_a.shape[1]
    d_b = w_up_b.shape[1]
    n_proj = w_in.shape[2]
    assert n_proj == 4 * d_a + 4 * d_b and w_merge.shape[2] == 2 * d
    assert n_proj % (2 * d) == 0 and seq % CHUNK == 0
    x2 = x.reshape(bsz * seq, d)
    for l in range(depth):
        mod3 = _mod_call(c, w_ada[l], b_ada[l]).reshape(bsz, 3, d)
        pcat = _proj_call(x2, mod3, g_pre[l:l + 1], w_in[l].astype(BF16), w_merge[l].astype(BF16),
                          b_merge[l:l + 1], conv_w[l], conv_b[l:l + 1], seq)
        x2 = _mix_out_call(pcat, lb_logits, g_head_a[l:l + 1], x2, mod3, w_up_a[l].astype(BF16),
                           w_up_b[l].astype(BF16), w_o[l].astype(BF16), g_post[l:l + 1], seq, l)
    return x2.reshape(bsz, seq, d)
```

```python
import functools

import jax
import jax.numpy as jnp
from jax import lax
from jax.experimental import pallas as pl
from jax.experimental.pallas import tpu as pltpu

F32 = jnp.float32
BF16 = jnp.bfloat16

EPS = 1e-6
CHUNK = 64
A_HEADS = 8
CONV_W = 3
CONV_PAD = 8

VMEM_LIMIT_BYTES = 58 * 1024 * 1024


def _sigmoid(v):
    return 1.0 / (1.0 + jnp.exp(-v))


def _mod_kernel(c_ref, w_ref, b_ref, o_ref):
    c = c_ref[...]
    c_act = c * _sigmoid(c)
    acc = jnp.dot(c_act.astype(BF16), w_ref[...].astype(BF16), preferred_element_type=F32)
    o_ref[...] = acc + b_ref[...]


def _mod_call(c, w_ada, b_ada, tn=1024):
    bsz, d = c.shape
    n = w_ada.shape[1]
    return pl.pallas_call(
        _mod_kernel,
        grid=(n // tn,),
        in_specs=[
            pl.BlockSpec((bsz, d), lambda j: (0, 0)),
            pl.BlockSpec((d, tn), lambda j: (0, j)),
            pl.BlockSpec((1, tn), lambda j: (0, j)),
        ],
        out_specs=pl.BlockSpec((bsz, tn), lambda j: (0, j)),
        out_shape=jax.ShapeDtypeStruct((bsz, n), F32),
        compiler_params=pltpu.CompilerParams(
            dimension_semantics=("arbitrary",), vmem_limit_bytes=VMEM_LIMIT_BYTES),
        name="adaln_mod",
    )(c, w_ada, b_ada.reshape(1, n))


def _cast_kernel(*refs):
    n = len(refs) // 2
    for src, dst in zip(refs[:n], refs[n:]):
        dst[...] = src[...].astype(BF16)


def _cast_call(weights, n_steps=16):
    specs = [pl.BlockSpec((w.shape[0] // n_steps, w.shape[1]), lambda s: (s, 0)) for w in weights]
    return pl.pallas_call(
        _cast_kernel,
        grid=(n_steps,),
        in_specs=specs,
        out_specs=specs,
        out_shape=[jax.ShapeDtypeStruct(w.shape, BF16) for w in weights],
        compiler_params=pltpu.CompilerParams(
            dimension_semantics=("arbitrary",), vmem_limit_bytes=VMEM_LIMIT_BYTES),
        name="cast_out_weights",
    )(*weights)


def _prenorm_rows(xs_ref, mod_ref, g_ref, h_ref, r_src, r_dst, n_rows):
    xv = xs_ref[r_src:r_src + n_rows, :]
    ms = jnp.mean(xv * xv, axis=-1, keepdims=True)
    y = xv * lax.rsqrt(ms + EPS) * g_ref[...]
    h_ref[pl.ds(r_dst, n_rows), :] = (y * (1.0 + mod_ref[0, 1:2, :]) + mod_ref[0, 0:1, :]).astype(BF16)


def _proj_kernel(xs_ref, mod_ref, g_ref, wi_ref, wm_ref, b_ref, o_ref, h0_ref, h1_ref,
                 *, n_proj_tiles, sub_rows):
    i = pl.program_id(0)
    j = pl.program_id(1)
    slab_rows = xs_ref.shape[0]

    def prenorm_slab(h_ref):
        for s in range(slab_rows // sub_rows):
            r_dst = pl.multiple_of(j * slab_rows + s * sub_rows, sub_rows)
            _prenorm_rows(xs_ref, mod_ref, g_ref, h_ref, s * sub_rows, r_dst, sub_rows)

    @pl.when(jnp.logical_and(i == 0, j < n_proj_tiles))
    def _():
        prenorm_slab(h0_ref)

    def step(h_read, h_write):
        @pl.when(j < n_proj_tiles)
        def _():
            acc = jnp.dot(h_read[...], wi_ref[...], preferred_element_type=F32)
            o_ref[...] = acc.astype(BF16)
            prenorm_slab(h_write)

        @pl.when(j >= n_proj_tiles)
        def _():
            acc = jnp.dot(h_read[...], wm_ref[...], preferred_element_type=F32)
            o_ref[...] = _sigmoid(acc + b_ref[...]).astype(BF16)

    @pl.when(jnp.logical_and(i > 0, i % 2 == 1))
    def _():
        step(h0_ref, h1_ref)

    @pl.when(jnp.logical_and(i > 0, i % 2 == 0))
    def _():
        step(h1_ref, h0_ref)


def _proj_call(x2, mod3, g_pre, w_in, w_merge, b_merge, seq, tm=1024, tn=2048):
    t, d = x2.shape
    n_proj = w_in.shape[1]
    n_merge = w_merge.shape[1]
    n_i = t // tm
    n_proj_tiles = n_proj // tn
    slab = tm // n_proj_tiles
    kern = functools.partial(_proj_kernel, n_proj_tiles=n_proj_tiles, sub_rows=32)
    cur = lambda i: jnp.minimum(i, n_i - 1)
    return pl.pallas_call(
        kern,
        grid=(n_i + 1, (n_proj + n_merge) // tn),
        in_specs=[
            pl.BlockSpec((slab, d), lambda i, j: (cur(i) * n_proj_tiles + jnp.minimum(j, n_proj_tiles - 1), 0)),
            pl.BlockSpec((1, 3, d), lambda i, j: ((cur(i) * tm) // seq, 0, 0)),
            pl.BlockSpec((1, d), lambda i, j: (0, 0)),
            pl.BlockSpec((d, tn), lambda i, j: (0, jnp.where(i > 0, jnp.minimum(j, n_proj_tiles - 1), 0))),
            pl.BlockSpec((d, tn), lambda i, j: (0, jnp.where(i > 0, jnp.maximum(j - n_proj_tiles, 0), 0))),
            pl.BlockSpec((1, tn), lambda i, j: (0, jnp.maximum(j - n_proj_tiles, 0))),
        ],
        out_specs=pl.BlockSpec((tm, tn), lambda i, j: (jnp.maximum(i - 1, 0), jnp.where(i > 0, j, 0))),
        out_shape=jax.ShapeDtypeStruct((t, n_proj + n_merge), BF16),
        scratch_shapes=[pltpu.VMEM((tm, d), BF16), pltpu.VMEM((tm, d), BF16)],
        compiler_params=pltpu.CompilerParams(
            dimension_semantics=("arbitrary", "arbitrary"), vmem_limit_bytes=VMEM_LIMIT_BYTES),
        name="prenorm_proj",
    )(x2, mod3, g_pre, w_in, w_merge, b_merge)


def _lower_bound(lbl_ref, layer):
    lbl = lbl_ref[...]
    lmax = jnp.max(lbl, axis=0, keepdims=True)
    le = jnp.exp(lbl - lmax)
    return (jnp.sum(le[0:layer + 1, :], axis=0, keepdims=True)
            / jnp.sum(le, axis=0, keepdims=True))


def _causal_mask():
    row = lax.broadcasted_iota(jnp.int32, (CHUNK, CHUNK), 0)
    col = lax.broadcasted_iota(jnp.int32, (CHUNK, CHUNK), 1)
    return col <= row


def _mix_unit(c, h, p_ref, lb, gh_ref, cw_ref, cb_ref, y_ref, state_ref, u_ref, *, d_a, d_b):
    r0 = c * CHUNK
    rows = slice(r0, r0 + CHUNK)
    dk = d_a // A_HEADS
    hs = slice(h * dk, (h + 1) * dk)
    causal = _causal_mask()
    tril = causal.astype(BF16)

    f_a = p_ref[rows, d_a + h * dk:d_a + (h + 1) * dk].astype(F32)
    lbh = lb[:, hs]
    f = lbh + (1.0 - lbh) * _sigmoid(f_a)
    log_f = jnp.log(f)
    k = 1.0 - f
    lf_hi = log_f.astype(BF16)
    lf_lo = (log_f - lf_hi.astype(F32)).astype(BF16)
    b2 = jnp.dot(tril, jnp.concatenate([lf_hi, lf_lo], axis=1), preferred_element_type=F32)

    base = 4 * d_a
    pad = CONV_PAD
    gate_cc = p_ref[rows, base + d_b + h * dk:base + d_b + (h + 1) * dk].astype(F32)
    v_b = p_ref[rows, base + 2 * d_b + h * dk:base + 2 * d_b + (h + 1) * dk].astype(F32)
    u_ref[pad + r0:pad + r0 + CHUNK, hs] = gate_cc * v_b
    conv = cb_ref[:, hs] + cw_ref[CONV_W - 1:CONV_W, hs] * u_ref[pad + r0:pad + r0 + CHUNK, hs]
    for jj in range(CONV_W - 1):
        back = CONV_W - 1 - jj
        conv = conv + cw_ref[jj:jj + 1, hs] * u_ref[pad + r0 - back:pad + r0 - back + CHUNK, hs]
    gate_bb = p_ref[rows, base + h * dk:base + (h + 1) * dk].astype(F32)
    z_b = p_ref[rows, base + 3 * d_b + h * dk:base + 3 * d_b + (h + 1) * dk].astype(F32)
    y_ref[rows, d_a + h * dk:d_a + (h + 1) * dk] = (
        gate_bb * conv * (z_b * _sigmoid(z_b))).astype(BF16)
    yield

    b = b2[:, 0:dk] + b2[:, dk:2 * dk]
    b_mid = b[CHUNK // 2 - 1:CHUNK // 2, :]
    b_last = b[CHUNK - 1:CHUNK, :]
    q = p_ref[rows, h * dk:(h + 1) * dk].astype(F32)
    q_rel = q * jnp.exp(b - b_mid)
    k_rel = k * jnp.exp(b_mid - b)
    q_abs = (q_rel * jnp.exp(b_mid)).astype(BF16)
    k_last = (k_rel * jnp.exp(b_last - b_mid)).astype(BF16)
    decay = jnp.exp(b_last)
    v = p_ref[rows, 2 * d_a + h * dk:2 * d_a + (h + 1) * dk]
    scores = lax.dot_general(q_rel.astype(BF16), k_rel.astype(BF16), (((1,), (1,)), ((), ())),
                             preferred_element_type=F32)
    d_st = lax.dot_general(v, k_last, (((0,), (0,)), ((), ())),
                           preferred_element_type=F32)
    yield

    scores = jnp.where(causal, scores, 0.0).astype(BF16)
    st = state_ref[h]
    o = jnp.dot(scores, v, preferred_element_type=F32)
    o = o + lax.dot_general(q_abs, st.astype(BF16), (((1,), (1,)), ((), ())),
                            preferred_element_type=F32)
    state_ref[h] = st * decay + d_st
    yield

    o = o * lax.rsqrt(jnp.mean(o * o, axis=-1, keepdims=True) + EPS) * gh_ref[...]
    z = p_ref[rows, 3 * d_a + h * dk:3 * d_a + (h + 1) * dk].astype(F32)
    y_ref[rows, hs] = (o * (z * _sigmoid(z))).astype(BF16)


MIX_UNIT_PHASES = 4


def _merge_piece(j, nc, y_ref, m_ref, wa_ref, wb_ref, mix_ref, *, d_a, d):
    cols = slice(j * nc, (j + 1) * nc)
    p_a = jnp.dot(y_ref[:, 0:d_a], wa_ref[:, cols], preferred_element_type=F32)
    p_b = jnp.dot(y_ref[:, d_a:], wb_ref[:, cols], preferred_element_type=F32)
    m_a = m_ref[:, j * nc:(j + 1) * nc].astype(F32)
    m_b = m_ref[:, d + j * nc:d + (j + 1) * nc].astype(F32)
    mix_ref[:, cols] = (m_a * p_a + m_b * p_b).astype(BF16)


def _wo_piece(j, nc, mix_ref, wo_ref, acc_ref):
    cols = slice(j * nc, (j + 1) * nc)
    acc_ref[:, cols] = jnp.dot(mix_ref[...], wo_ref[:, cols], preferred_element_type=F32)


def _post_piece(acc_ref, x_ref, mod_ref, g_ref, o_ref):
    out = acc_ref[...]
    normed = out * lax.rsqrt(jnp.mean(out * out, axis=-1, keepdims=True) + EPS) * g_ref[...]
    o_ref[...] = x_ref[...] + mod_ref[0, 2:3, :] * normed


def _mix_out_kernel(p_ref, lbl_ref, gh_ref, cw_ref, cb_ref, m_ref, x_ref, mod_ref, wa_ref, wb_ref,
                    wo_ref, g_ref, o_ref, state_ref, u_ref, y0_ref, y1_ref, mix_ref, acc_ref,
                    *, d_a, d_b, layer, n_blocks, blocks_per_seq, out_cols):
    k = pl.program_id(0)
    blk = jnp.minimum(k, n_blocks - 1)
    ts, d = x_ref.shape
    n_chunks = ts // CHUNK
    n_out = d // out_cols

    @pl.when(k == 0)
    def _():
        y1_ref[...] = jnp.zeros_like(y1_ref)

    @pl.when(blk % blocks_per_seq == 0)
    def _():
        state_ref[...] = jnp.zeros_like(state_ref)
        u_ref[0:CONV_PAD, :] = jnp.zeros((CONV_PAD, d_b), F32)

    def step(y_write, y_read):
        lb = _lower_bound(lbl_ref, layer)
        out_pieces = (
            [functools.partial(_merge_piece, j, out_cols, y_read, m_ref, wa_ref, wb_ref, mix_ref,
                               d_a=d_a, d=d) for j in range(n_out)]
            + [functools.partial(_wo_piece, j, out_cols, mix_ref, wo_ref, acc_ref)
               for j in range(n_out)])
        mix_units = [[_mix_unit(c, h, p_ref, lb, gh_ref, cw_ref, cb_ref, y_write, state_ref, u_ref,
                                d_a=d_a, d_b=d_b) for h in range(A_HEADS)]
                     for c in range(n_chunks)]
        groups = [t - ph for t in range(n_chunks + MIX_UNIT_PHASES - 1)
                  for ph in reversed(range(MIX_UNIT_PHASES)) if 0 <= t - ph < n_chunks]
        for gi, c in enumerate(groups):
            lo = (gi * len(out_pieces)) // len(groups)
            hi = ((gi + 1) * len(out_pieces)) // len(groups)
            for piece in out_pieces[lo:hi]:
                piece()
            for unit in mix_units[c]:
                next(unit, None)
        u_ref[0:CONV_PAD, :] = u_ref[ts:ts + CONV_PAD, :]
        _post_piece(acc_ref, x_ref, mod_ref, g_ref, o_ref)

    @pl.when(k % 2 == 0)
    def _():
        step(y0_ref, y1_ref)

    @pl.when(k % 2 == 1)
    def _():
        step(y1_ref, y0_ref)


def _mix_out_call(pcat, lb_logits, g_head, conv_w, conv_b, x2, mod3, w_up_a, w_up_b, w_o, g_post,
                  seq, layer, ts=256):
    t, d = x2.shape
    d_a = w_up_a.shape[0]
    d_b = w_up_b.shape[0]
    n_in = 4 * d_a + 4 * d_b
    dk = d_a // A_HEADS
    n_blocks = t // ts
    m_col_block = n_in // (2 * d)
    kern = functools.partial(_mix_out_kernel, d_a=d_a, d_b=d_b, layer=layer, n_blocks=n_blocks,
                             blocks_per_seq=seq // ts, out_cols=512)
    const = lambda k: (0, 0)
    cur = lambda k: jnp.minimum(k, n_blocks - 1)
    prev = lambda k: jnp.maximum(k - 1, 0)
    return pl.pallas_call(
        kern,
        grid=(n_blocks + 1,),
        in_specs=[
            pl.BlockSpec((ts, n_in), lambda k: (cur(k), 0)),
            pl.BlockSpec(lb_logits.shape, const),
            pl.BlockSpec((1, dk), const),
            pl.BlockSpec((CONV_W, d_b), const),
            pl.BlockSpec((1, d_b), const),
            pl.BlockSpec((ts, 2 * d), lambda k: (prev(k), m_col_block)),
            pl.BlockSpec((ts, d), lambda k: (prev(k), 0)),
            pl.BlockSpec((1, 3, d), lambda k: ((prev(k) * ts) // seq, 0, 0)),
            pl.BlockSpec((d_a, d), const, pipeline_mode=pl.Buffered(1)),
            pl.BlockSpec((d_b, d), const, pipeline_mode=pl.Buffered(1)),
            pl.BlockSpec((d, d), const, pipeline_mode=pl.Buffered(1)),
            pl.BlockSpec((1, d), const),
        ],
        out_specs=pl.BlockSpec((ts, d), lambda k: (prev(k), 0)),
        out_shape=jax.ShapeDtypeStruct((t, d), F32),
        scratch_shapes=[
            pltpu.VMEM((A_HEADS, dk, dk), F32),
            pltpu.VMEM((ts + CONV_PAD, d_b), F32),
            pltpu.VMEM((ts, d_a + d_b), BF16),
            pltpu.VMEM((ts, d_a + d_b), BF16),
            pltpu.VMEM((ts, d), BF16),
            pltpu.VMEM((ts, d), F32),
        ],
        compiler_params=pltpu.CompilerParams(
            dimension_semantics=("arbitrary",), vmem_limit_bytes=VMEM_LIMIT_BYTES),
        name="mix_merge_out",
    )(pcat, lb_logits, g_head, conv_w, conv_b, pcat, x2, mod3, w_up_a, w_up_b, w_o, g_post)


def kernel(x, c, w_ada, b_ada, g_pre, w_in, lb_logits, g_head_a, conv_w, conv_b,
           w_up_a, w_up_b, w_merge, b_merge, w_o, g_post):
    bsz, seq, d = x.shape
    depth = w_in.shape[0]
    d_a = w_up_a.shape[1]
    d_b = w_up_b.shape[1]
    n_proj = w_in.shape[2]
    assert n_proj == 4 * d_a + 4 * d_b and w_merge.shape[2] == 2 * d
    assert n_proj % (2 * d) == 0 and seq % CHUNK == 0
    x2 = x.reshape(bsz * seq, d)
    for l in range(depth):
        mod3 = _mod_call(c, w_ada[l], b_ada[l]).reshape(bsz, 3, d)
        pcat = _proj_call(x2, mod3, g_pre[l:l + 1], w_in[l].astype(BF16), w_merge[l].astype(BF16),
                          b_merge[l:l + 1], seq)
        wa16, wb16, wo16 = _cast_call([w_up_a[l], w_up_b[l], w_o[l]])
        x2 = _mix_out_call(pcat, lb_logits, g_head_a[l:l + 1], conv_w[l], conv_b[l:l + 1], x2, mod3,
                           wa16, wb16, wo16, g_post[l:l + 1], seq, l)
    return x2.reshape(bsz, seq, d)
```

```python
import functools

import jax
import jax.numpy as jnp
from jax import lax
from jax.experimental import pallas as pl
from jax.experimental.pallas import tpu as pltpu

F32 = jnp.float32
BF16 = jnp.bfloat16

EPS = 1e-6
CHUNK = 64
A_HEADS = 8
CONV_W = 3
CONV_PAD = 8

VMEM_LIMIT_BYTES = 58 * 1024 * 1024


def _sigmoid(v):
    return 1.0 / (1.0 + jnp.exp(-v))


def _mod_kernel(c_ref, w_ref, b_ref, o_ref):
    c = c_ref[...]
    c_act = c * _sigmoid(c)
    acc = jnp.dot(c_act.astype(BF16), w_ref[...].astype(BF16), preferred_element_type=F32)
    o_ref[...] = acc + b_ref[...]


def _mod_call(c, w_ada, b_ada, tn=1024):
    bsz, d = c.shape
    n = w_ada.shape[1]
    return pl.pallas_call(
        _mod_kernel,
        grid=(n // tn,),
        in_specs=[
            pl.BlockSpec((bsz, d), lambda j: (0, 0)),
            pl.BlockSpec((d, tn), lambda j: (0, j)),
            pl.BlockSpec((1, tn), lambda j: (0, j)),
        ],
        out_specs=pl.BlockSpec((bsz, tn), lambda j: (0, j)),
        out_shape=jax.ShapeDtypeStruct((bsz, n), F32),
        compiler_params=pltpu.CompilerParams(
            dimension_semantics=("arbitrary",), vmem_limit_bytes=VMEM_LIMIT_BYTES),
        name="adaln_mod",
    )(c, w_ada, b_ada.reshape(1, n))


def _cast_kernel(*refs):
    n = len(refs) // 2
    for src, dst in zip(refs[:n], refs[n:]):
        dst[...] = src[...].astype(BF16)


def _cast_call(weights, n_steps=16):
    specs = [pl.BlockSpec((w.shape[0] // n_steps, w.shape[1]), lambda s: (s, 0)) for w in weights]
    return pl.pallas_call(
        _cast_kernel,
        grid=(n_steps,),
        in_specs=specs,
        out_specs=specs,
        out_shape=[jax.ShapeDtypeStruct(w.shape, BF16) for w in weights],
        compiler_params=pltpu.CompilerParams(
            dimension_semantics=("arbitrary",), vmem_limit_bytes=VMEM_LIMIT_BYTES),
        name="cast_out_weights",
    )(*weights)


def _prenorm_rows(xs_ref, mod_ref, g_ref, h_ref, r_src, r_dst, n_rows):
    xv = xs_ref[r_src:r_src + n_rows, :]
    ms = jnp.mean(xv * xv, axis=-1, keepdims=True)
    y = xv * lax.rsqrt(ms + EPS) * g_ref[...]
    h_ref[pl.ds(r_dst, n_rows), :] = (y * (1.0 + mod_ref[0, 1:2, :]) + mod_ref[0, 0:1, :]).astype(BF16)


def _proj_kernel(xs_ref, mod_ref, g_ref, wi_ref, wm_ref, b_ref, o_ref, h0_ref, h1_ref,
                 *, n_proj_tiles, sub_rows):
    i = pl.program_id(0)
    j = pl.program_id(1)
    slab_rows = xs_ref.shape[0]

    def prenorm_slab(h_ref):
        for s in range(slab_rows // sub_rows):
            r_dst = pl.multiple_of(j * slab_rows + s * sub_rows, sub_rows)
            _prenorm_rows(xs_ref, mod_ref, g_ref, h_ref, s * sub_rows, r_dst, sub_rows)

    @pl.when(jnp.logical_and(i == 0, j < n_proj_tiles))
    def _():
        prenorm_slab(h0_ref)

    def step(h_read, h_write):
        @pl.when(j < n_proj_tiles)
        def _():
            acc = jnp.dot(h_read[...], wi_ref[...], preferred_element_type=F32)
            o_ref[...] = acc.astype(BF16)
            prenorm_slab(h_write)

        @pl.when(j >= n_proj_tiles)
        def _():
            acc = jnp.dot(h_read[...], wm_ref[...], preferred_element_type=F32)
            o_ref[...] = _sigmoid(acc + b_ref[...]).astype(BF16)

    @pl.when(jnp.logical_and(i > 0, i % 2 == 1))
    def _():
        step(h0_ref, h1_ref)

    @pl.when(jnp.logical_and(i > 0, i % 2 == 0))
    def _():
        step(h1_ref, h0_ref)


def _proj_call(x2, mod3, g_pre, w_in, w_merge, b_merge, seq, tm=1024, tn=2048):
    t, d = x2.shape
    n_proj = w_in.shape[1]
    n_merge = w_merge.shape[1]
    n_i = t // tm
    n_proj_tiles = n_proj // tn
    slab = tm // n_proj_tiles
    kern = functools.partial(_proj_kernel, n_proj_tiles=n_proj_tiles, sub_rows=32)
    cur = lambda i: jnp.minimum(i, n_i - 1)
    return pl.pallas_call(
        kern,
        grid=(n_i + 1, (n_proj + n_merge) // tn),
        in_specs=[
            pl.BlockSpec((slab, d), lambda i, j: (cur(i) * n_proj_tiles + jnp.minimum(j, n_proj_tiles - 1), 0)),
            pl.BlockSpec((1, 3, d), lambda i, j: ((cur(i) * tm) // seq, 0, 0)),
            pl.BlockSpec((1, d), lambda i, j: (0, 0)),
            pl.BlockSpec((d, tn), lambda i, j: (0, jnp.where(i > 0, jnp.minimum(j, n_proj_tiles - 1), 0))),
            pl.BlockSpec((d, tn), lambda i, j: (0, jnp.where(i > 0, jnp.maximum(j - n_proj_tiles, 0), 0))),
            pl.BlockSpec((1, tn), lambda i, j: (0, jnp.maximum(j - n_proj_tiles, 0))),
        ],
        out_specs=pl.BlockSpec((tm, tn), lambda i, j: (jnp.maximum(i - 1, 0), jnp.where(i > 0, j, 0))),
        out_shape=jax.ShapeDtypeStruct((t, n_proj + n_merge), BF16),
        scratch_shapes=[pltpu.VMEM((tm, d), BF16), pltpu.VMEM((tm, d), BF16)],
        compiler_params=pltpu.CompilerParams(
            dimension_semantics=("arbitrary", "arbitrary"), vmem_limit_bytes=VMEM_LIMIT_BYTES),
        name="prenorm_proj",
    )(x2, mod3, g_pre, w_in, w_merge, b_merge)


def _lower_bound(lbl_ref, layer):
    lbl = lbl_ref[...]
    lmax = jnp.max(lbl, axis=0, keepdims=True)
    le = jnp.exp(lbl - lmax)
    return (jnp.sum(le[0:layer + 1, :], axis=0, keepdims=True)
            / jnp.sum(le, axis=0, keepdims=True))


def _causal_mask():
    row = lax.broadcasted_iota(jnp.int32, (CHUNK, CHUNK), 0)
    col = lax.broadcasted_iota(jnp.int32, (CHUNK, CHUNK), 1)
    return col <= row


def _mix_unit(c, h, p_ref, lb, gh_ref, cw_ref, cb_ref, y_ref, state_ref, u_ref, *, d_a, d_b):
    r0 = c * CHUNK
    rows = slice(r0, r0 + CHUNK)
    dk = d_a // A_HEADS
    hs = slice(h * dk, (h + 1) * dk)
    causal = _causal_mask()
    tril = causal.astype(BF16)

    f_a = p_ref[rows, d_a + h * dk:d_a + (h + 1) * dk].astype(F32)
    lbh = lb[:, hs]
    f = lbh + (1.0 - lbh) * _sigmoid(f_a)
    log_f = jnp.log(f)
    k = 1.0 - f
    lf_hi = log_f.astype(BF16)
    lf_lo = (log_f - lf_hi.astype(F32)).astype(BF16)
    b2 = jnp.dot(tril, jnp.concatenate([lf_hi, lf_lo], axis=1), preferred_element_type=F32)

    base = 4 * d_a
    pad = CONV_PAD
    gate_cc = p_ref[rows, base + d_b + h * dk:base + d_b + (h + 1) * dk].astype(F32)
    v_b = p_ref[rows, base + 2 * d_b + h * dk:base + 2 * d_b + (h + 1) * dk].astype(F32)
    u_ref[pad + r0:pad + r0 + CHUNK, hs] = gate_cc * v_b
    conv = cb_ref[:, hs] + cw_ref[CONV_W - 1:CONV_W, hs] * u_ref[pad + r0:pad + r0 + CHUNK, hs]
    for jj in range(CONV_W - 1):
        back = CONV_W - 1 - jj
        conv = conv + cw_ref[jj:jj + 1, hs] * u_ref[pad + r0 - back:pad + r0 - back + CHUNK, hs]
    gate_bb = p_ref[rows, base + h * dk:base + (h + 1) * dk].astype(F32)
    z_b = p_ref[rows, base + 3 * d_b + h * dk:base + 3 * d_b + (h + 1) * dk].astype(F32)
    y_ref[rows, d_a + h * dk:d_a + (h + 1) * dk] = (
        gate_bb * conv * (z_b * _sigmoid(z_b))).astype(BF16)
    yield

    b = b2[:, 0:dk] + b2[:, dk:2 * dk]
    b_mid = b[CHUNK // 2 - 1:CHUNK // 2, :]
    b_last = b[CHUNK - 1:CHUNK, :]
    q = p_ref[rows, h * dk:(h + 1) * dk].astype(F32)
    q_rel = q * jnp.exp(b - b_mid)
    k_rel = k * jnp.exp(b_mid - b)
    q_abs = (q_rel * jnp.exp(b_mid)).astype(BF16)
    k_last = (k_rel * jnp.exp(b_last - b_mid)).astype(BF16)
    decay = jnp.exp(b_last)
    v = p_ref[rows, 2 * d_a + h * dk:2 * d_a + (h + 1) * dk]
    v_t = v.astype(F32).T.astype(BF16)
    scores = lax.dot_general(q_rel.astype(BF16), k_rel.astype(BF16), (((1,), (1,)), ((), ())),
                             preferred_element_type=F32)
    d_st = jnp.dot(v_t, k_last, preferred_element_type=F32)
    yield

    scores = jnp.where(causal, scores, 0.0).astype(BF16)
    st = state_ref[h]
    o = lax.dot_general(jnp.concatenate([q_abs, scores], axis=1),
                        jnp.concatenate([st.astype(BF16), v_t], axis=1),
                        (((1,), (1,)), ((), ())), preferred_element_type=F32)
    state_ref[h] = st * decay + d_st
    yield

    o = o * lax.rsqrt(jnp.mean(o * o, axis=-1, keepdims=True) + EPS) * gh_ref[...]
    z = p_ref[rows, 3 * d_a + h * dk:3 * d_a + (h + 1) * dk].astype(F32)
    y_ref[rows, hs] = (o * (z * _sigmoid(z))).astype(BF16)


MIX_UNIT_PHASES = 4


def _merge_piece(j, nc, y_ref, m_ref, wa_ref, wb_ref, mix_ref, *, d_a, d):
    cols = slice(j * nc, (j + 1) * nc)
    p_a = jnp.dot(y_ref[:, 0:d_a], wa_ref[:, cols], preferred_element_type=F32)
    p_b = jnp.dot(y_ref[:, d_a:], wb_ref[:, cols], preferred_element_type=F32)
    m_a = m_ref[:, j * nc:(j + 1) * nc].astype(F32)
    m_b = m_ref[:, d + j * nc:d + (j + 1) * nc].astype(F32)
    mix_ref[:, cols] = (m_a * p_a + m_b * p_b).astype(BF16)


def _wo_piece(j, nc, mix_ref, wo_ref, acc_ref):
    cols = slice(j * nc, (j + 1) * nc)
    acc_ref[:, cols] = jnp.dot(mix_ref[...], wo_ref[:, cols], preferred_element_type=F32)


def _post_piece(acc_ref, x_ref, mod_ref, g_ref, o_ref):
    out = acc_ref[...]
    normed = out * lax.rsqrt(jnp.mean(out * out, axis=-1, keepdims=True) + EPS) * g_ref[...]
    o_ref[...] = x_ref[...] + mod_ref[0, 2:3, :] * normed


def _mix_out_kernel(p_ref, lbl_ref, gh_ref, cw_ref, cb_ref, m_ref, x_ref, mod_ref, wa_ref, wb_ref,
                    wo_ref, g_ref, o_ref, state_ref, u_ref, y0_ref, y1_ref, mix_ref, acc_ref,
                    *, d_a, d_b, layer, n_blocks, blocks_per_seq, out_cols):
    k = pl.program_id(0)
    blk = jnp.minimum(k, n_blocks - 1)
    ts, d = x_ref.shape
    n_chunks = ts // CHUNK
    n_out = d // out_cols

    @pl.when(k == 0)
    def _():
        y1_ref[...] = jnp.zeros_like(y1_ref)

    @pl.when(blk % blocks_per_seq == 0)
    def _():
        state_ref[...] = jnp.zeros_like(state_ref)
        u_ref[0:CONV_PAD, :] = jnp.zeros((CONV_PAD, d_b), F32)

    def step(y_write, y_read):
        lb = _lower_bound(lbl_ref, layer)
        out_pieces = (
            [functools.partial(_merge_piece, j, out_cols, y_read, m_ref, wa_ref, wb_ref, mix_ref,
                               d_a=d_a, d=d) for j in range(n_out)]
            + [functools.partial(_wo_piece, j, out_cols, mix_ref, wo_ref, acc_ref)
               for j in range(n_out)])
        mix_units = [[_mix_unit(c, h, p_ref, lb, gh_ref, cw_ref, cb_ref, y_write, state_ref, u_ref,
                                d_a=d_a, d_b=d_b) for h in range(A_HEADS)]
                     for c in range(n_chunks)]
        groups = [t - ph for t in range(n_chunks + MIX_UNIT_PHASES - 1)
                  for ph in reversed(range(MIX_UNIT_PHASES)) if 0 <= t - ph < n_chunks]
        for gi, c in enumerate(groups):
            lo = (gi * len(out_pieces)) // len(groups)
            hi = ((gi + 1) * len(out_pieces)) // len(groups)
            for piece in out_pieces[lo:hi]:
                piece()
            for unit in mix_units[c]:
                next(unit, None)
        u_ref[0:CONV_PAD, :] = u_ref[ts:ts + CONV_PAD, :]
        _post_piece(acc_ref, x_ref, mod_ref, g_ref, o_ref)

    @pl.when(k % 2 == 0)
    def _():
        step(y0_ref, y1_ref)

    @pl.when(k % 2 == 1)
    def _():
        step(y1_ref, y0_ref)


def _mix_out_call(pcat, lb_logits, g_head, conv_w, conv_b, x2, mod3, w_up_a, w_up_b, w_o, g_post,
                  seq, layer, ts=256):
    t, d = x2.shape
    d_a = w_up_a.shape[0]
    d_b = w_up_b.shape[0]
    n_in = 4 * d_a + 4 * d_b
    dk = d_a // A_HEADS
    n_blocks = t // ts
    m_col_block = n_in // (2 * d)
    kern = functools.partial(_mix_out_kernel, d_a=d_a, d_b=d_b, layer=layer, n_blocks=n_blocks,
                             blocks_per_seq=seq // ts, out_cols=512)
    const = lambda k: (0, 0)
    cur = lambda k: jnp.minimum(k, n_blocks - 1)
    prev = lambda k: jnp.maximum(k - 1, 0)
    return pl.pallas_call(
        kern,
        grid=(n_blocks + 1,),
        in_specs=[
            pl.BlockSpec((ts, n_in), lambda k: (cur(k), 0)),
            pl.BlockSpec(lb_logits.shape, const),
            pl.BlockSpec((1, dk), const),
            pl.BlockSpec((CONV_W, d_b), const),
            pl.BlockSpec((1, d_b), const),
            pl.BlockSpec((ts, 2 * d), lambda k: (prev(k), m_col_block)),
            pl.BlockSpec((ts, d), lambda k: (prev(k), 0)),
            pl.BlockSpec((1, 3, d), lambda k: ((prev(k) * ts) // seq, 0, 0)),
            pl.BlockSpec((d_a, d), const, pipeline_mode=pl.Buffered(1)),
            pl.BlockSpec((d_b, d), const, pipeline_mode=pl.Buffered(1)),
            pl.BlockSpec((d, d), const, pipeline_mode=pl.Buffered(1)),
            pl.BlockSpec((1, d), const),
        ],
        out_specs=pl.BlockSpec((ts, d), lambda k: (prev(k), 0)),
        out_shape=jax.ShapeDtypeStruct((t, d), F32),
        scratch_shapes=[
            pltpu.VMEM((A_HEADS, dk, dk), F32),
            pltpu.VMEM((ts + CONV_PAD, d_b), F32),
            pltpu.VMEM((ts, d_a + d_b), BF16),
            pltpu.VMEM((ts, d_a + d_b), BF16),
            pltpu.VMEM((ts, d), BF16),
            pltpu.VMEM((ts, d), F32),
        ],
        compiler_params=pltpu.CompilerParams(
            dimension_semantics=("arbitrary",), vmem_limit_bytes=VMEM_LIMIT_BYTES),
        name="mix_merge_out",
    )(pcat, lb_logits, g_head, conv_w, conv_b, pcat, x2, mod3, w_up_a, w_up_b, w_o, g_post)


def kernel(x, c, w_ada, b_ada, g_pre, w_in, lb_logits, g_head_a, conv_w, conv_b,
           w_up_a, w_up_b, w_merge, b_merge, w_o, g_post):
    bsz, seq, d = x.shape
    depth = w_in.shape[0]
    d_a = w_up_a.shape[1]
    d_b = w_up_b.shape[1]
    n_proj = w_in.shape[2]
    assert n_proj == 4 * d_a + 4 * d_b and w_merge.shape[2] == 2 * d
    assert n_proj % (2 * d) == 0 and seq % CHUNK == 0
    x2 = x.reshape(bsz * seq, d)
    for l in range(depth):
        mod3 = _mod_call(c, w_ada[l], b_ada[l]).reshape(bsz, 3, d)
        pcat = _proj_call(x2, mod3, g_pre[l:l + 1], w_in[l].astype(BF16), w_merge[l].astype(BF16),
                          b_merge[l:l + 1], seq)
        wa16, wb16, wo16 = _cast_call([w_up_a[l], w_up_b[l], w_o[l]])
        x2 = _mix_out_call(pcat, lb_logits, g_head_a[l:l + 1], conv_w[l], conv_b[l:l + 1], x2, mod3,
                           wa16, wb16, wo16, g_post[l:l + 1], seq, l)
    return x2.reshape(bsz, seq, d)
```

```python
import functools

import jax
import jax.numpy as jnp
from jax import lax
from jax.experimental import pallas as pl
from jax.experimental.pallas import tpu as pltpu

F32 = jnp.float32
BF16 = jnp.bfloat16

EPS = 1e-6
CHUNK = 64
A_HEADS = 8
CONV_W = 3
CONV_PAD = 8

VMEM_LIMIT_BYTES = 58 * 1024 * 1024


def _sigmoid(v):
    return 1.0 / (1.0 + jnp.exp(-v))


def _mod_kernel(c_ref, w_ref, b_ref, o_ref):
    c = c_ref[...]
    c_act = c * _sigmoid(c)
    acc = jnp.dot(c_act.astype(BF16), w_ref[...].astype(BF16), preferred_element_type=F32)
    o_ref[...] = acc + b_ref[...]


def _mod_call(c, w_ada, b_ada, tn=1024):
    bsz, d = c.shape
    n = w_ada.shape[1]
    return pl.pallas_call(
        _mod_kernel,
        grid=(n // tn,),
        in_specs=[
            pl.BlockSpec((bsz, d), lambda j: (0, 0)),
            pl.BlockSpec((d, tn), lambda j: (0, j)),
            pl.BlockSpec((1, tn), lambda j: (0, j)),
        ],
        out_specs=pl.BlockSpec((bsz, tn), lambda j: (0, j)),
        out_shape=jax.ShapeDtypeStruct((bsz, n), F32),
        compiler_params=pltpu.CompilerParams(
            dimension_semantics=("arbitrary",), vmem_limit_bytes=VMEM_LIMIT_BYTES),
        name="adaln_mod",
    )(c, w_ada, b_ada.reshape(1, n))


def _cast_kernel(*refs):
    n = len(refs) // 2
    for src, dst in zip(refs[:n], refs[n:]):
        dst[...] = src[...].astype(BF16)


def _cast_call(weights, n_steps=16):
    specs = [pl.BlockSpec((w.shape[0] // n_steps, w.shape[1]), lambda s: (s, 0)) for w in weights]
    return pl.pallas_call(
        _cast_kernel,
        grid=(n_steps,),
        in_specs=specs,
        out_specs=specs,
        out_shape=[jax.ShapeDtypeStruct(w.shape, BF16) for w in weights],
        compiler_params=pltpu.CompilerParams(
            dimension_semantics=("arbitrary",), vmem_limit_bytes=VMEM_LIMIT_BYTES),
        name="cast_out_weights",
    )(*weights)


def _prenorm_rows(xs_ref, mod_ref, g_ref, h_ref, r_src, r_dst, n_rows):
    xv = xs_ref[r_src:r_src + n_rows, :]
    ms = jnp.mean(xv * xv, axis=-1, keepdims=True)
    y = xv * lax.rsqrt(ms + EPS) * g_ref[...]
    h_ref[pl.ds(r_dst, n_rows), :] = (y * (1.0 + mod_ref[0, 1:2, :]) + mod_ref[0, 0:1, :]).astype(BF16)


def _proj_kernel(xs_ref, mod_ref, g_ref, wi_ref, wm_ref, b_ref, o_ref, h0_ref, h1_ref,
                 *, n_proj_tiles, sub_rows):
    i = pl.program_id(0)
    j = pl.program_id(1)
    slab_rows = xs_ref.shape[0]

    def prenorm_slab(h_ref):
        for s in range(slab_rows // sub_rows):
            r_dst = pl.multiple_of(j * slab_rows + s * sub_rows, sub_rows)
            _prenorm_rows(xs_ref, mod_ref, g_ref, h_ref, s * sub_rows, r_dst, sub_rows)

    @pl.when(jnp.logical_and(i == 0, j < n_proj_tiles))
    def _():
        prenorm_slab(h0_ref)

    def step(h_read, h_write):
        @pl.when(j < n_proj_tiles)
        def _():
            acc = jnp.dot(h_read[...], wi_ref[...], preferred_element_type=F32)
            o_ref[...] = acc.astype(BF16)
            prenorm_slab(h_write)

        @pl.when(j >= n_proj_tiles)
        def _():
            acc = jnp.dot(h_read[...], wm_ref[...], preferred_element_type=F32)
            o_ref[...] = _sigmoid(acc + b_ref[...]).astype(BF16)

    @pl.when(jnp.logical_and(i > 0, i % 2 == 1))
    def _():
        step(h0_ref, h1_ref)

    @pl.when(jnp.logical_and(i > 0, i % 2 == 0))
    def _():
        step(h1_ref, h0_ref)


def _proj_call(x2, mod3, g_pre, w_in, w_merge, b_merge, seq, tm=1024, tn=2048):
    t, d = x2.shape
    n_proj = w_in.shape[1]
    n_merge = w_merge.shape[1]
    n_i = t // tm
    n_proj_tiles = n_proj // tn
    slab = tm // n_proj_tiles
    kern = functools.partial(_proj_kernel, n_proj_tiles=n_proj_tiles, sub_rows=128)
    cur = lambda i: jnp.minimum(i, n_i - 1)
    return pl.pallas_call(
        kern,
        grid=(n_i + 1, (n_proj + n_merge) // tn),
        in_specs=[
            pl.BlockSpec((slab, d), lambda i, j: (cur(i) * n_proj_tiles + jnp.minimum(j, n_proj_tiles - 1), 0)),
            pl.BlockSpec((1, 3, d), lambda i, j: ((cur(i) * tm) // seq, 0, 0)),
            pl.BlockSpec((1, d), lambda i, j: (0, 0)),
            pl.BlockSpec((d, tn), lambda i, j: (0, jnp.where(i > 0, jnp.minimum(j, n_proj_tiles - 1), 0))),
            pl.BlockSpec((d, tn), lambda i, j: (0, jnp.where(i > 0, jnp.maximum(j - n_proj_tiles, 0), 0))),
            pl.BlockSpec((1, tn), lambda i, j: (0, jnp.maximum(j - n_proj_tiles, 0))),
        ],
        out_specs=pl.BlockSpec((tm, tn), lambda i, j: (jnp.maximum(i - 1, 0), jnp.where(i > 0, j, 0))),
        out_shape=jax.ShapeDtypeStruct((t, n_proj + n_merge), BF16),
        scratch_shapes=[pltpu.VMEM((tm, d), BF16), pltpu.VMEM((tm, d), BF16)],
        compiler_params=pltpu.CompilerParams(
            dimension_semantics=("arbitrary", "arbitrary"), vmem_limit_bytes=VMEM_LIMIT_BYTES),
        name="prenorm_proj",
    )(x2, mod3, g_pre, w_in, w_merge, b_merge)


def _lower_bound(lbl_ref, layer):
    lbl = lbl_ref[...]
    lmax = jnp.max(lbl, axis=0, keepdims=True)
    le = jnp.exp(lbl - lmax)
    return (jnp.sum(le[0:layer + 1, :], axis=0, keepdims=True)
            / jnp.sum(le, axis=0, keepdims=True))


def _causal_mask():
    row = lax.broadcasted_iota(jnp.int32, (CHUNK, CHUNK), 0)
    col = lax.broadcasted_iota(jnp.int32, (CHUNK, CHUNK), 1)
    return col <= row


def _mix_unit(c, h, p_ref, lb, gh_ref, cw_ref, cb_ref, y_ref, state_ref, u_ref, *, d_a, d_b):
    r0 = c * CHUNK
    rows = slice(r0, r0 + CHUNK)
    dk = d_a // A_HEADS
    hs = slice(h * dk, (h + 1) * dk)
    causal = _causal_mask()
    tril = causal.astype(BF16)

    f_a = p_ref[rows, d_a + h * dk:d_a + (h + 1) * dk].astype(F32)
    lbh = lb[:, hs]
    f = lbh + (1.0 - lbh) * _sigmoid(f_a)
    log_f = jnp.log(f)
    k = 1.0 - f
    lf_hi = log_f.astype(BF16)
    lf_lo = (log_f - lf_hi.astype(F32)).astype(BF16)
    b2 = jnp.dot(tril, jnp.concatenate([lf_hi, lf_lo], axis=1), preferred_element_type=F32)

    base = 4 * d_a
    pad = CONV_PAD
    gate_cc = p_ref[rows, base + d_b + h * dk:base + d_b + (h + 1) * dk].astype(F32)
    v_b = p_ref[rows, base + 2 * d_b + h * dk:base + 2 * d_b + (h + 1) * dk].astype(F32)
    u_ref[pad + r0:pad + r0 + CHUNK, hs] = gate_cc * v_b
    conv = cb_ref[:, hs] + cw_ref[CONV_W - 1:CONV_W, hs] * u_ref[pad + r0:pad + r0 + CHUNK, hs]
    for jj in range(CONV_W - 1):
        back = CONV_W - 1 - jj
        conv = conv + cw_ref[jj:jj + 1, hs] * u_ref[pad + r0 - back:pad + r0 - back + CHUNK, hs]
    gate_bb = p_ref[rows, base + h * dk:base + (h + 1) * dk].astype(F32)
    z_b = p_ref[rows, base + 3 * d_b + h * dk:base + 3 * d_b + (h + 1) * dk].astype(F32)
    y_ref[rows, d_a + h * dk:d_a + (h + 1) * dk] = (
        gate_bb * conv * (z_b * _sigmoid(z_b))).astype(BF16)
    yield

    b = b2[:, 0:dk] + b2[:, dk:2 * dk]
    b_mid = b[CHUNK // 2 - 1:CHUNK // 2, :]
    b_last = b[CHUNK - 1:CHUNK, :]
    q = p_ref[rows, h * dk:(h + 1) * dk].astype(F32)
    e_rel = jnp.exp(b - b_mid)
    q_rel = q * e_rel
    k_rel = k / e_rel
    q_abs = (q_rel * jnp.exp(b_mid)).astype(BF16)
    k_last = (k_rel * jnp.exp(b_last - b_mid)).astype(BF16)
    decay = jnp.exp(b_last)
    v = p_ref[rows, 2 * d_a + h * dk:2 * d_a + (h + 1) * dk]
    v_t = v.astype(F32).T.astype(BF16)
    scores = lax.dot_general(q_rel.astype(BF16), k_rel.astype(BF16), (((1,), (1,)), ((), ())),
                             preferred_element_type=F32)
    scores = jnp.where(causal, scores, 0.0).astype(BF16)
    yield

    d_st = jnp.dot(v_t, k_last, preferred_element_type=F32)
    st = state_ref[h]
    o = lax.dot_general(jnp.concatenate([q_abs, scores], axis=1),
                        jnp.concatenate([st.astype(BF16), v_t], axis=1),
                        (((1,), (1,)), ((), ())), preferred_element_type=F32)
    state_ref[h] = st * decay + d_st
    yield

    o = o * lax.rsqrt(jnp.mean(o * o, axis=-1, keepdims=True) + EPS) * gh_ref[...]
    z = p_ref[rows, 3 * d_a + h * dk:3 * d_a + (h + 1) * dk].astype(F32)
    y_ref[rows, hs] = (o * (z * _sigmoid(z))).astype(BF16)


MIX_UNIT_PHASES = 4


def _merge_piece(j, nc, y_ref, m_ref, wa_ref, wb_ref, mix_ref, *, d_a, d):
    cols = slice(j * nc, (j + 1) * nc)
    p_a = jnp.dot(y_ref[:, 0:d_a], wa_ref[:, cols], preferred_element_type=F32)
    p_b = jnp.dot(y_ref[:, d_a:], wb_ref[:, cols], preferred_element_type=F32)
    m_a = m_ref[:, j * nc:(j + 1) * nc].astype(F32)
    m_b = m_ref[:, d + j * nc:d + (j + 1) * nc].astype(F32)
    mix_ref[:, cols] = (m_a * p_a + m_b * p_b).astype(BF16)


def _wo_piece(j, nc, mix_ref, wo_ref, acc_ref):
    cols = slice(j * nc, (j + 1) * nc)
    acc_ref[:, cols] = jnp.dot(mix_ref[...], wo_ref[:, cols], preferred_element_type=F32)


def _post_piece(acc_ref, x_ref, mod_ref, g_ref, o_ref):
    out = acc_ref[...]
    normed = out * lax.rsqrt(jnp.mean(out * out, axis=-1, keepdims=True) + EPS) * g_ref[...]
    o_ref[...] = x_ref[...] + mod_ref[0, 2:3, :] * normed


def _mix_out_kernel(p_ref, lbl_ref, gh_ref, cw_ref, cb_ref, m_ref, x_ref, mod_ref, wa_ref, wb_ref,
                    wo_ref, g_ref, o_ref, state_ref, u_ref, y0_ref, y1_ref, mix_ref, acc_ref,
                    *, d_a, d_b, layer, n_blocks, blocks_per_seq, out_cols):
    k = pl.program_id(0)
    blk = jnp.minimum(k, n_blocks - 1)
    ts, d = x_ref.shape
    n_chunks = ts // CHUNK
    n_out = d // out_cols

    @pl.when(blk % blocks_per_seq == 0)
    def _():
        state_ref[...] = jnp.zeros_like(state_ref)
        u_ref[0:CONV_PAD, :] = jnp.zeros((CONV_PAD, d_b), F32)

    def step(y_write, y_read):
        out_pieces = []
        if y_read is not None:
            out_pieces = (
                [functools.partial(_merge_piece, j, out_cols, y_read, m_ref, wa_ref, wb_ref, mix_ref,
                                   d_a=d_a, d=d) for j in range(n_out)]
                + [functools.partial(_wo_piece, j, out_cols, mix_ref, wo_ref, acc_ref)
                   for j in range(n_out)])
        if y_write is None:
            for piece in out_pieces:
                piece()
        else:
            lb = _lower_bound(lbl_ref, layer)
            mix_units = [[_mix_unit(c, h, p_ref, lb, gh_ref, cw_ref, cb_ref, y_write, state_ref, u_ref,
                                    d_a=d_a, d_b=d_b) for h in range(A_HEADS)]
                         for c in range(n_chunks)]
            groups = [t - ph for t in range(n_chunks + MIX_UNIT_PHASES - 1)
                      for ph in reversed(range(MIX_UNIT_PHASES)) if 0 <= t - ph < n_chunks]
            for gi, c in enumerate(groups):
                lo = (gi * len(out_pieces)) // len(groups)
                hi = ((gi + 1) * len(out_pieces)) // len(groups)
                for piece in out_pieces[lo:hi]:
                    piece()
                for unit in mix_units[c]:
                    next(unit, None)
            u_ref[0:CONV_PAD, :] = u_ref[ts:ts + CONV_PAD, :]
        if y_read is not None:
            _post_piece(acc_ref, x_ref, mod_ref, g_ref, o_ref)

    y_refs = (y0_ref, y1_ref)
    interior = jnp.logical_and(k > 0, k < n_blocks)

    @pl.when(k == 0)
    def _():
        step(y_refs[0], None)

    @pl.when(k == n_blocks)
    def _():
        step(None, y_refs[(n_blocks + 1) % 2])

    @pl.when(jnp.logical_and(interior, k % 2 == 0))
    def _():
        step(y_refs[0], y_refs[1])

    @pl.when(jnp.logical_and(interior, k % 2 == 1))
    def _():
        step(y_refs[1], y_refs[0])


def _mix_out_call(pcat, lb_logits, g_head, conv_w, conv_b, x2, mod3, w_up_a, w_up_b, w_o, g_post,
                  seq, layer, ts=256):
    t, d = x2.shape
    d_a = w_up_a.shape[0]
    d_b = w_up_b.shape[0]
    n_in = 4 * d_a + 4 * d_b
    dk = d_a // A_HEADS
    n_blocks = t // ts
    m_col_block = n_in // (2 * d)
    kern = functools.partial(_mix_out_kernel, d_a=d_a, d_b=d_b, layer=layer, n_blocks=n_blocks,
                             blocks_per_seq=seq // ts, out_cols=512)
    const = lambda k: (0, 0)
    cur = lambda k: jnp.minimum(k, n_blocks - 1)
    prev = lambda k: jnp.maximum(k - 1, 0)
    return pl.pallas_call(
        kern,
        grid=(n_blocks + 1,),
        in_specs=[
            pl.BlockSpec((ts, n_in), lambda k: (cur(k), 0)),
            pl.BlockSpec(lb_logits.shape, const),
            pl.BlockSpec((1, dk), const),
            pl.BlockSpec((CONV_W, d_b), const),
            pl.BlockSpec((1, d_b), const),
            pl.BlockSpec((ts, 2 * d), lambda k: (prev(k), m_col_block)),
            pl.BlockSpec((ts, d), lambda k: (prev(k), 0)),
            pl.BlockSpec((1, 3, d), lambda k: ((prev(k) * ts) // seq, 0, 0)),
            pl.BlockSpec((d_a, d), const, pipeline_mode=pl.Buffered(1)),
            pl.BlockSpec((d_b, d), const, pipeline_mode=pl.Buffered(1)),
            pl.BlockSpec((d, d), const, pipeline_mode=pl.Buffered(1)),
            pl.BlockSpec((1, d), const),
        ],
        out_specs=pl.BlockSpec((ts, d), lambda k: (prev(k), 0)),
        out_shape=jax.ShapeDtypeStruct((t, d), F32),
        scratch_shapes=[
            pltpu.VMEM((A_HEADS, dk, dk), F32),
            pltpu.VMEM((ts + CONV_PAD, d_b), F32),
            pltpu.VMEM((ts, d_a + d_b), BF16),
            pltpu.VMEM((ts, d_a + d_b), BF16),
            pltpu.VMEM((ts, d), BF16),
            pltpu.VMEM((ts, d), F32),
        ],
        compiler_params=pltpu.CompilerParams(
            dimension_semantics=("arbitrary",), vmem_limit_bytes=VMEM_LIMIT_BYTES),
        name="mix_merge_out",
    )(pcat, lb_logits, g_head, conv_w, conv_b, pcat, x2, mod3, w_up_a, w_up_b, w_o, g_post)


def kernel(x, c, w_ada, b_ada, g_pre, w_in, lb_logits, g_head_a, conv_w, conv_b,
           w_up_a, w_up_b, w_merge, b_merge, w_o, g_post):
    bsz, seq, d = x.shape
    depth = w_in.shape[0]
    d_a = w_up_a.shape[1]
    d_b = w_up_b.shape[1]
    n_proj = w_in.shape[2]
    assert n_proj == 4 * d_a + 4 * d_b and w_merge.shape[2] == 2 * d
    assert n_proj % (2 * d) == 0 and seq % CHUNK == 0
    x2 = x.reshape(bsz * seq, d)
    for l in range(depth):
        mod3 = _mod_call(c, w_ada[l], b_ada[l]).reshape(bsz, 3, d)
        pcat = _proj_call(x2, mod3, g_pre[l:l + 1], w_in[l].astype(BF16), w_merge[l].astype(BF16),
                          b_merge[l:l + 1], seq)
        wa16, wb16, wo16 = _cast_call([w_up_a[l], w_up_b[l], w_o[l]])
        x2 = _mix_out_call(pcat, lb_logits, g_head_a[l:l + 1], conv_w[l], conv_b[l:l + 1], x2, mod3,
                           wa16, wb16, wo16, g_post[l:l + 1], seq, l)
    return x2.reshape(bsz, seq, d)
```

```python
import functools

import jax
import jax.numpy as jnp
from jax import lax
from jax.experimental import pallas as pl
from jax.experimental.pallas import tpu as pltpu

F32 = jnp.float32
BF16 = jnp.bfloat16

EPS = 1e-6
CHUNK = 64
A_HEADS = 8
CONV_W = 3
CONV_PAD = 8

VMEM_LIMIT_BYTES = 58 * 1024 * 1024


def _sigmoid(v):
    return 1.0 / (1.0 + jnp.exp(-v))


def _mod_kernel(c_ref, w_ref, b_ref, o_ref):
    c = c_ref[...]
    c_act = c * _sigmoid(c)
    acc = jnp.dot(c_act.astype(BF16), w_ref[...].astype(BF16), preferred_element_type=F32)
    o_ref[...] = acc + b_ref[...]


def _mod_call(c, w_ada, b_ada, tn=1024):
    bsz, d = c.shape
    n = w_ada.shape[1]
    return pl.pallas_call(
        _mod_kernel,
        grid=(n // tn,),
        in_specs=[
            pl.BlockSpec((bsz, d), lambda j: (0, 0)),
            pl.BlockSpec((d, tn), lambda j: (0, j)),
            pl.BlockSpec((1, tn), lambda j: (0, j)),
        ],
        out_specs=pl.BlockSpec((bsz, tn), lambda j: (0, j)),
        out_shape=jax.ShapeDtypeStruct((bsz, n), F32),
        compiler_params=pltpu.CompilerParams(
            dimension_semantics=("arbitrary",), vmem_limit_bytes=VMEM_LIMIT_BYTES),
        name="adaln_mod",
    )(c, w_ada, b_ada.reshape(1, n))


def _cast_kernel(*refs):
    n = len(refs) // 2
    for src, dst in zip(refs[:n], refs[n:]):
        dst[...] = src[...].astype(BF16)


def _cast_call(weights, n_steps=16):
    specs = [pl.BlockSpec((w.shape[0] // n_steps, w.shape[1]), lambda s: (s, 0)) for w in weights]
    return pl.pallas_call(
        _cast_kernel,
        grid=(n_steps,),
        in_specs=specs,
        out_specs=specs,
        out_shape=[jax.ShapeDtypeStruct(w.shape, BF16) for w in weights],
        compiler_params=pltpu.CompilerParams(
            dimension_semantics=("arbitrary",), vmem_limit_bytes=VMEM_LIMIT_BYTES),
        name="cast_out_weights",
    )(*weights)


def _prenorm_rows(xs_ref, mod_ref, g_ref, h_ref, r_src, r_dst, n_rows):
    xv = xs_ref[r_src:r_src + n_rows, :]
    ms = jnp.mean(xv * xv, axis=-1, keepdims=True)
    y = xv * lax.rsqrt(ms + EPS) * g_ref[...]
    h_ref[pl.ds(r_dst, n_rows), :] = (y * (1.0 + mod_ref[0, 1:2, :]) + mod_ref[0, 0:1, :]).astype(BF16)


def _proj_kernel(xs_ref, mod_ref, g_ref, wi_ref, wm_ref, b_ref, o_ref, h0_ref, h1_ref,
                 *, n_proj_tiles, sub_rows):
    i = pl.program_id(0)
    j = pl.program_id(1)
    slab_rows = xs_ref.shape[0]

    def prenorm_slab(h_ref):
        for s in range(slab_rows // sub_rows):
            r_dst = pl.multiple_of(j * slab_rows + s * sub_rows, sub_rows)
            _prenorm_rows(xs_ref, mod_ref, g_ref, h_ref, s * sub_rows, r_dst, sub_rows)

    @pl.when(jnp.logical_and(i == 0, j < n_proj_tiles))
    def _():
        prenorm_slab(h0_ref)

    def step(h_read, h_write):
        @pl.when(j < n_proj_tiles)
        def _():
            acc = jnp.dot(h_read[...], wi_ref[...], preferred_element_type=F32)
            o_ref[...] = acc.astype(BF16)
            prenorm_slab(h_write)

        @pl.when(j >= n_proj_tiles)
        def _():
            acc = jnp.dot(h_read[...], wm_ref[...], preferred_element_type=F32)
            o_ref[...] = _sigmoid(acc + b_ref[...]).astype(BF16)

    @pl.when(jnp.logical_and(i > 0, i % 2 == 1))
    def _():
        step(h0_ref, h1_ref)

    @pl.when(jnp.logical_and(i > 0, i % 2 == 0))
    def _():
        step(h1_ref, h0_ref)


def _proj_call(x2, mod3, g_pre, w_in, w_merge, b_merge, seq, tm=1024, tn=2048):
    t, d = x2.shape
    n_proj = w_in.shape[1]
    n_merge = w_merge.shape[1]
    n_i = t // tm
    n_proj_tiles = n_proj // tn
    slab = tm // n_proj_tiles
    kern = functools.partial(_proj_kernel, n_proj_tiles=n_proj_tiles, sub_rows=128)
    cur = lambda i: jnp.minimum(i, n_i - 1)
    return pl.pallas_call(
        kern,
        grid=(n_i + 1, (n_proj + n_merge) // tn),
        in_specs=[
            pl.BlockSpec((slab, d), lambda i, j: (cur(i) * n_proj_tiles + jnp.minimum(j, n_proj_tiles - 1), 0)),
            pl.BlockSpec((1, 3, d), lambda i, j: ((cur(i) * tm) // seq, 0, 0)),
            pl.BlockSpec((1, d), lambda i, j: (0, 0)),
            pl.BlockSpec((d, tn), lambda i, j: (0, jnp.where(i > 0, jnp.minimum(j, n_proj_tiles - 1), 0))),
            pl.BlockSpec((d, tn), lambda i, j: (0, jnp.where(i > 0, jnp.maximum(j - n_proj_tiles, 0), 0))),
            pl.BlockSpec((1, tn), lambda i, j: (0, jnp.maximum(j - n_proj_tiles, 0))),
        ],
        out_specs=pl.BlockSpec((tm, tn), lambda i, j: (jnp.maximum(i - 1, 0), jnp.where(i > 0, j, 0))),
        out_shape=jax.ShapeDtypeStruct((t, n_proj + n_merge), BF16),
        scratch_shapes=[pltpu.VMEM((tm, d), BF16), pltpu.VMEM((tm, d), BF16)],
        compiler_params=pltpu.CompilerParams(
            dimension_semantics=("arbitrary", "arbitrary"), vmem_limit_bytes=VMEM_LIMIT_BYTES),
        name="prenorm_proj",
    )(x2, mod3, g_pre, w_in, w_merge, b_merge)


def _lower_bound(lbl_ref, layer):
    lbl = lbl_ref[...]
    lmax = jnp.max(lbl, axis=0, keepdims=True)
    le = jnp.exp(lbl - lmax)
    return (jnp.sum(le[0:layer + 1, :], axis=0, keepdims=True)
            / jnp.sum(le, axis=0, keepdims=True))


def _causal_mask():
    row = lax.broadcasted_iota(jnp.int32, (CHUNK, CHUNK), 0)
    col = lax.broadcasted_iota(jnp.int32, (CHUNK, CHUNK), 1)
    return col <= row


def _mix_unit(c, h, p_ref, lb, gh_ref, cw_ref, cb_ref, y_ref, state_ref, u_ref, *, d_a, d_b):
    r0 = c * CHUNK
    rows = slice(r0, r0 + CHUNK)
    dk = d_a // A_HEADS
    hs = slice(h * dk, (h + 1) * dk)
    causal = _causal_mask()
    tril = causal.astype(BF16)

    f_a = p_ref[rows, d_a + h * dk:d_a + (h + 1) * dk].astype(F32)
    lbh = lb[:, hs]
    f = lbh + (1.0 - lbh) * _sigmoid(f_a)
    log_f = jnp.log(f)
    k = 1.0 - f
    lf_hi = log_f.astype(BF16)
    lf_lo = (log_f - lf_hi.astype(F32)).astype(BF16)
    b2 = jnp.dot(tril, jnp.concatenate([lf_hi, lf_lo], axis=1), preferred_element_type=F32)

    base = 4 * d_a
    pad = CONV_PAD
    gate_cc = p_ref[rows, base + d_b + h * dk:base + d_b + (h + 1) * dk].astype(F32)
    v_b = p_ref[rows, base + 2 * d_b + h * dk:base + 2 * d_b + (h + 1) * dk].astype(F32)
    u_ref[pad + r0:pad + r0 + CHUNK, hs] = gate_cc * v_b
    conv = cb_ref[:, hs] + cw_ref[CONV_W - 1:CONV_W, hs] * u_ref[pad + r0:pad + r0 + CHUNK, hs]
    for jj in range(CONV_W - 1):
        back = CONV_W - 1 - jj
        conv = conv + cw_ref[jj:jj + 1, hs] * u_ref[pad + r0 - back:pad + r0 - back + CHUNK, hs]
    gate_bb = p_ref[rows, base + h * dk:base + (h + 1) * dk].astype(F32)
    z_b = p_ref[rows, base + 3 * d_b + h * dk:base + 3 * d_b + (h + 1) * dk].astype(F32)
    y_ref[rows, d_a + h * dk:d_a + (h + 1) * dk] = (
        gate_bb * conv * (z_b * _sigmoid(z_b))).astype(BF16)
    yield

    b = b2[:, 0:dk] + b2[:, dk:2 * dk]
    b_mid = b[CHUNK // 2 - 1:CHUNK // 2, :]
    b_last = b[CHUNK - 1:CHUNK, :]
    q = p_ref[rows, h * dk:(h + 1) * dk].astype(F32)
    e_rel = jnp.exp(b - b_mid)
    q_rel = q * e_rel
    k_rel = k / e_rel
    q_abs = (q_rel * jnp.exp(b_mid)).astype(BF16)
    k_last = (k_rel * jnp.exp(b_last - b_mid)).astype(BF16)
    decay = jnp.exp(b_last)
    v = p_ref[rows, 2 * d_a + h * dk:2 * d_a + (h + 1) * dk]
    v_t = v.astype(F32).T.astype(BF16)
    scores = lax.dot_general(q_rel.astype(BF16), k_rel.astype(BF16), (((1,), (1,)), ((), ())),
                             preferred_element_type=F32)
    d_st = jnp.dot(v_t, k_last, preferred_element_type=F32)
    yield

    scores = jnp.where(causal, scores, 0.0).astype(BF16)
    st = state_ref[h]
    o = lax.dot_general(jnp.concatenate([q_abs, scores], axis=1),
                        jnp.concatenate([st.astype(BF16), v_t], axis=1),
                        (((1,), (1,)), ((), ())), preferred_element_type=F32)
    state_ref[h] = st * decay + d_st
    yield

    o = o * lax.rsqrt(jnp.mean(o * o, axis=-1, keepdims=True) + EPS) * gh_ref[...]
    z = p_ref[rows, 3 * d_a + h * dk:3 * d_a + (h + 1) * dk].astype(F32)
    y_ref[rows, hs] = (o * (z * _sigmoid(z))).astype(BF16)


MIX_UNIT_PHASES = 4


def _merge_piece(j, nc, y_ref, m_ref, wa_ref, wb_ref, mix_ref, *, d_a, d):
    cols = slice(j * nc, (j + 1) * nc)
    p_a = jnp.dot(y_ref[:, 0:d_a], wa_ref[:, cols], preferred_element_type=F32)
    p_b = jnp.dot(y_ref[:, d_a:], wb_ref[:, cols], preferred_element_type=F32)
    m_a = m_ref[:, j * nc:(j + 1) * nc].astype(F32)
    m_b = m_ref[:, d + j * nc:d + (j + 1) * nc].astype(F32)
    mix_ref[:, cols] = (m_a * p_a + m_b * p_b).astype(BF16)


def _wo_piece(j, nc, mix_ref, wo_ref, acc_ref):
    cols = slice(j * nc, (j + 1) * nc)
    acc_ref[:, cols] = jnp.dot(mix_ref[...], wo_ref[:, cols], preferred_element_type=F32)


def _post_piece(acc_ref, x_ref, mod_ref, g_ref, o_ref):
    out = acc_ref[...]
    normed = out * lax.rsqrt(jnp.mean(out * out, axis=-1, keepdims=True) + EPS) * g_ref[...]
    o_ref[...] = x_ref[...] + mod_ref[0, 2:3, :] * normed


def _mix_out_kernel(p_ref, lbl_ref, gh_ref, cw_ref, cb_ref, m_ref, x_ref, mod_ref, wa_ref, wb_ref,
                    wo_ref, g_ref, o_ref, state_ref, u_ref, y0_ref, y1_ref, mix_ref, acc_ref,
                    *, d_a, d_b, layer, n_blocks, blocks_per_seq, out_cols):
    k = pl.program_id(0)
    blk = jnp.minimum(k, n_blocks - 1)
    ts, d = x_ref.shape
    n_chunks = ts // CHUNK
    n_out = d // out_cols

    @pl.when(blk % blocks_per_seq == 0)
    def _():
        state_ref[...] = jnp.zeros_like(state_ref)
        u_ref[0:CONV_PAD, :] = jnp.zeros((CONV_PAD, d_b), F32)

    def step(y_write, y_read):
        out_pieces = []
        if y_read is not None:
            out_pieces = (
                [functools.partial(_merge_piece, j, out_cols, y_read, m_ref, wa_ref, wb_ref, mix_ref,
                                   d_a=d_a, d=d) for j in range(n_out)]
                + [functools.partial(_wo_piece, j, out_cols, mix_ref, wo_ref, acc_ref)
                   for j in range(n_out)])
        if y_write is None:
            for piece in out_pieces:
                piece()
        else:
            lb = _lower_bound(lbl_ref, layer)
            mix_units = [[_mix_unit(c, h, p_ref, lb, gh_ref, cw_ref, cb_ref, y_write, state_ref, u_ref,
                                    d_a=d_a, d_b=d_b) for h in range(A_HEADS)]
                         for c in range(n_chunks)]
            groups = [t - ph for t in range(n_chunks + MIX_UNIT_PHASES - 1)
                      for ph in reversed(range(MIX_UNIT_PHASES)) if 0 <= t - ph < n_chunks]
            for gi, c in enumerate(groups):
                lo = (gi * len(out_pieces)) // len(groups)
                hi = ((gi + 1) * len(out_pieces)) // len(groups)
                for piece in out_pieces[lo:hi]:
                    piece()
                for unit in mix_units[c]:
                    next(unit, None)
            u_ref[0:CONV_PAD, :] = u_ref[ts:ts + CONV_PAD, :]
        if y_read is not None:
            _post_piece(acc_ref, x_ref, mod_ref, g_ref, o_ref)

    y_refs = (y0_ref, y1_ref)
    interior = jnp.logical_and(k > 0, k < n_blocks)

    @pl.when(k == 0)
    def _():
        step(y_refs[0], None)

    @pl.when(k == n_blocks)
    def _():
        step(None, y_refs[(n_blocks + 1) % 2])

    @pl.when(jnp.logical_and(interior, k % 2 == 0))
    def _():
        step(y_refs[0], y_refs[1])

    @pl.when(jnp.logical_and(interior, k % 2 == 1))
    def _():
        step(y_refs[1], y_refs[0])


def _mix_out_call(pcat, lb_logits, g_head, conv_w, conv_b, x2, mod3, w_up_a, w_up_b, w_o, g_post,
                  seq, layer, ts=256):
    t, d = x2.shape
    d_a = w_up_a.shape[0]
    d_b = w_up_b.shape[0]
    n_in = 4 * d_a + 4 * d_b
    dk = d_a // A_HEADS
    n_blocks = t // ts
    m_col_block = n_in // (2 * d)
    kern = functools.partial(_mix_out_kernel, d_a=d_a, d_b=d_b, layer=layer, n_blocks=n_blocks,
                             blocks_per_seq=seq // ts, out_cols=1024)
    const = lambda k: (0, 0)
    cur = lambda k: jnp.minimum(k, n_blocks - 1)
    prev = lambda k: jnp.maximum(k - 1, 0)
    return pl.pallas_call(
        kern,
        grid=(n_blocks + 1,),
        in_specs=[
            pl.BlockSpec((ts, n_in), lambda k: (cur(k), 0)),
            pl.BlockSpec(lb_logits.shape, const),
            pl.BlockSpec((1, dk), const),
            pl.BlockSpec((CONV_W, d_b), const),
            pl.BlockSpec((1, d_b), const),
            pl.BlockSpec((ts, 2 * d), lambda k: (prev(k), m_col_block)),
            pl.BlockSpec((ts, d), lambda k: (prev(k), 0)),
            pl.BlockSpec((1, 3, d), lambda k: ((prev(k) * ts) // seq, 0, 0)),
            pl.BlockSpec((d_a, d), const, pipeline_mode=pl.Buffered(1)),
            pl.BlockSpec((d_b, d), const, pipeline_mode=pl.Buffered(1)),
            pl.BlockSpec((d, d), const, pipeline_mode=pl.Buffered(1)),
            pl.BlockSpec((1, d), const),
        ],
        out_specs=pl.BlockSpec((ts, d), lambda k: (prev(k), 0)),
        out_shape=jax.ShapeDtypeStruct((t, d), F32),
        scratch_shapes=[
            pltpu.VMEM((A_HEADS, dk, dk), F32),
            pltpu.VMEM((ts + CONV_PAD, d_b), F32),
            pltpu.VMEM((ts, d_a + d_b), BF16),
            pltpu.VMEM((ts, d_a + d_b), BF16),
            pltpu.VMEM((ts, d), BF16),
            pltpu.VMEM((ts, d), F32),
        ],
        compiler_params=pltpu.CompilerParams(
            dimension_semantics=("arbitrary",), vmem_limit_bytes=VMEM_LIMIT_BYTES),
        name="mix_merge_out",
    )(pcat, lb_logits, g_head, conv_w, conv_b, pcat, x2, mod3, w_up_a, w_up_b, w_o, g_post)


def kernel(x, c, w_ada, b_ada, g_pre, w_in, lb_logits, g_head_a, conv_w, conv_b,
           w_up_a, w_up_b, w_merge, b_merge, w_o, g_post):
    bsz, seq, d = x.shape
    depth = w_in.shape[0]
    d_a = w_up_a.shape[1]
    d_b = w_up_b.shape[1]
    n_proj = w_in.shape[2]
    assert n_proj == 4 * d_a + 4 * d_b and w_merge.shape[2] == 2 * d
    assert n_proj % (2 * d) == 0 and seq % CHUNK == 0
    x2 = x.reshape(bsz * seq, d)
    for l in range(depth):
        mod3 = _mod_call(c, w_ada[l], b_ada[l]).reshape(bsz, 3, d)
        pcat = _proj_call(x2, mod3, g_pre[l:l + 1], w_in[l].astype(BF16), w_merge[l].astype(BF16),
                          b_merge[l:l + 1], seq)
        wa16, wb16, wo16 = _cast_call([w_up_a[l], w_up_b[l], w_o[l]])
        x2 = _mix_out_call(pcat, lb_logits, g_head_a[l:l + 1], conv_w[l], conv_b[l:l + 1], x2, mod3,
                           wa16, wb16, wo16, g_post[l:l + 1], seq, l)
    return x2.reshape(bsz, seq, d)
```

```python
import functools

import jax
import jax.numpy as jnp
from jax import lax
from jax.experimental import pallas as pl
from jax.experimental.pallas import tpu as pltpu

F32 = jnp.float32
BF16 = jnp.bfloat16

EPS = 1e-6
CHUNK = 64
A_HEADS = 8
CONV_W = 3
CONV_PAD = 8

VMEM_LIMIT_BYTES = 58 * 1024 * 1024


def _sigmoid(v):
    return 1.0 / (1.0 + jnp.exp(-v))


def _mod_kernel(c_ref, w_ref, b_ref, o_ref):
    c = c_ref[...]
    c_act = c * _sigmoid(c)
    acc = jnp.dot(c_act.astype(BF16), w_ref[...].astype(BF16), preferred_element_type=F32)
    o_ref[...] = acc + b_ref[...]


def _mod_call(c, w_ada, b_ada, tn=1024):
    bsz, d = c.shape
    n = w_ada.shape[1]
    return pl.pallas_call(
        _mod_kernel,
        grid=(n // tn,),
        in_specs=[
            pl.BlockSpec((bsz, d), lambda j: (0, 0)),
            pl.BlockSpec((d, tn), lambda j: (0, j)),
            pl.BlockSpec((1, tn), lambda j: (0, j)),
        ],
        out_specs=pl.BlockSpec((bsz, tn), lambda j: (0, j)),
        out_shape=jax.ShapeDtypeStruct((bsz, n), F32),
        compiler_params=pltpu.CompilerParams(
            dimension_semantics=("arbitrary",), vmem_limit_bytes=VMEM_LIMIT_BYTES),
        name="adaln_mod",
    )(c, w_ada, b_ada.reshape(1, n))


def _cast_kernel(*refs):
    n = len(refs) // 2
    for src, dst in zip(refs[:n], refs[n:]):
        dst[...] = src[...].astype(BF16)


def _cast_call(weights, n_steps=16):
    specs = [pl.BlockSpec((w.shape[0] // n_steps, w.shape[1]), lambda s: (s, 0)) for w in weights]
    return pl.pallas_call(
        _cast_kernel,
        grid=(n_steps,),
        in_specs=specs,
        out_specs=specs,
        out_shape=[jax.ShapeDtypeStruct(w.shape, BF16) for w in weights],
        compiler_params=pltpu.CompilerParams(
            dimension_semantics=("arbitrary",), vmem_limit_bytes=VMEM_LIMIT_BYTES),
        name="cast_out_weights",
    )(*weights)


def _prenorm_rows(xs_ref, mod_ref, g_ref, h_ref, r_src, r_dst, n_rows):
    xv = xs_ref[r_src:r_src + n_rows, :]
    ms = jnp.mean(xv * xv, axis=-1, keepdims=True)
    gain = g_ref[...] * (1.0 + mod_ref[0, 1:2, :])
    h_ref[pl.ds(r_dst, n_rows), :] = ((xv * lax.rsqrt(ms + EPS)) * gain + mod_ref[0, 0:1, :]).astype(BF16)


def _proj_kernel(xs_ref, mod_ref, g_ref, wi_ref, wm_ref, b_ref, o_ref, h0_ref, h1_ref,
                 *, n_proj_tiles, sub_rows):
    i = pl.program_id(0)
    j = pl.program_id(1)
    slab_rows = xs_ref.shape[0]

    def prenorm_slab(h_ref):
        for s in range(slab_rows // sub_rows):
            r_dst = pl.multiple_of(j * slab_rows + s * sub_rows, sub_rows)
            _prenorm_rows(xs_ref, mod_ref, g_ref, h_ref, s * sub_rows, r_dst, sub_rows)

    @pl.when(jnp.logical_and(i == 0, j < n_proj_tiles))
    def _():
        prenorm_slab(h0_ref)

    def step(h_read, h_write):
        @pl.when(j < n_proj_tiles)
        def _():
            acc = jnp.dot(h_read[...], wi_ref[...], preferred_element_type=F32)
            o_ref[...] = acc.astype(BF16)
            prenorm_slab(h_write)

        @pl.when(j >= n_proj_tiles)
        def _():
            acc = jnp.dot(h_read[...], wm_ref[...], preferred_element_type=F32)
            o_ref[...] = _sigmoid(acc + b_ref[...]).astype(BF16)

    @pl.when(jnp.logical_and(i > 0, i % 2 == 1))
    def _():
        step(h0_ref, h1_ref)

    @pl.when(jnp.logical_and(i > 0, i % 2 == 0))
    def _():
        step(h1_ref, h0_ref)


def _proj_call(x2, mod3, g_pre, w_in, w_merge, b_merge, seq, tm=1024, tn=2048):
    t, d = x2.shape
    n_proj = w_in.shape[1]
    n_merge = w_merge.shape[1]
    n_i = t // tm
    n_proj_tiles = n_proj // tn
    slab = tm // n_proj_tiles
    kern = functools.partial(_proj_kernel, n_proj_tiles=n_proj_tiles, sub_rows=128)
    cur = lambda i: jnp.minimum(i, n_i - 1)
    return pl.pallas_call(
        kern,
        grid=(n_i + 1, (n_proj + n_merge) // tn),
        in_specs=[
            pl.BlockSpec((slab, d), lambda i, j: (cur(i) * n_proj_tiles + jnp.minimum(j, n_proj_tiles - 1), 0)),
            pl.BlockSpec((1, 3, d), lambda i, j: ((cur(i) * tm) // seq, 0, 0)),
            pl.BlockSpec((1, d), lambda i, j: (0, 0)),
            pl.BlockSpec((d, tn), lambda i, j: (0, jnp.where(i > 0, jnp.minimum(j, n_proj_tiles - 1), 0))),
            pl.BlockSpec((d, tn), lambda i, j: (0, jnp.where(i > 0, jnp.maximum(j - n_proj_tiles, 0), 0))),
            pl.BlockSpec((1, tn), lambda i, j: (0, jnp.maximum(j - n_proj_tiles, 0))),
        ],
        out_specs=pl.BlockSpec((tm, tn), lambda i, j: (jnp.maximum(i - 1, 0), jnp.where(i > 0, j, 0))),
        out_shape=jax.ShapeDtypeStruct((t, n_proj + n_merge), BF16),
        scratch_shapes=[pltpu.VMEM((tm, d), BF16), pltpu.VMEM((tm, d), BF16)],
        compiler_params=pltpu.CompilerParams(
            dimension_semantics=("arbitrary", "arbitrary"), vmem_limit_bytes=VMEM_LIMIT_BYTES),
        name="prenorm_proj",
    )(x2, mod3, g_pre, w_in, w_merge, b_merge)


def _lower_bound(lbl_ref, layer):
    lbl = lbl_ref[...]
    lmax = jnp.max(lbl, axis=0, keepdims=True)
    le = jnp.exp(lbl - lmax)
    return (jnp.sum(le[0:layer + 1, :], axis=0, keepdims=True)
            / jnp.sum(le, axis=0, keepdims=True))


def _causal_mask():
    row = lax.broadcasted_iota(jnp.int32, (CHUNK, CHUNK), 0)
    col = lax.broadcasted_iota(jnp.int32, (CHUNK, CHUNK), 1)
    return col <= row


def _mix_unit(c, h, p_ref, lb, gh_ref, cw_ref, cb_ref, y_ref, state_ref, u_ref, *, d_a, d_b):
    r0 = c * CHUNK
    rows = slice(r0, r0 + CHUNK)
    dk = d_a // A_HEADS
    hs = slice(h * dk, (h + 1) * dk)
    causal = _causal_mask()
    tril = causal.astype(BF16)

    f_a = p_ref[rows, d_a + h * dk:d_a + (h + 1) * dk].astype(F32)
    lbh = lb[:, hs]
    f = lbh + (1.0 - lbh) * _sigmoid(f_a)
    log_f = jnp.log(f)
    k = 1.0 - f
    lf_hi = log_f.astype(BF16)
    lf_lo = (log_f - lf_hi.astype(F32)).astype(BF16)
    b2 = jnp.dot(tril, jnp.concatenate([lf_hi, lf_lo], axis=1), preferred_element_type=F32)

    base = 4 * d_a
    pad = CONV_PAD
    gate_cc = p_ref[rows, base + d_b + h * dk:base + d_b + (h + 1) * dk].astype(F32)
    v_b = p_ref[rows, base + 2 * d_b + h * dk:base + 2 * d_b + (h + 1) * dk].astype(F32)
    u_ref[pad + r0:pad + r0 + CHUNK, hs] = gate_cc * v_b
    conv = cb_ref[:, hs] + cw_ref[CONV_W - 1:CONV_W, hs] * u_ref[pad + r0:pad + r0 + CHUNK, hs]
    for jj in range(CONV_W - 1):
        back = CONV_W - 1 - jj
        conv = conv + cw_ref[jj:jj + 1, hs] * u_ref[pad + r0 - back:pad + r0 - back + CHUNK, hs]
    gate_bb = p_ref[rows, base + h * dk:base + (h + 1) * dk].astype(F32)
    z_b = p_ref[rows, base + 3 * d_b + h * dk:base + 3 * d_b + (h + 1) * dk].astype(F32)
    y_ref[rows, d_a + h * dk:d_a + (h + 1) * dk] = (
        gate_bb * conv * (z_b * _sigmoid(z_b))).astype(BF16)
    yield

    b = b2[:, 0:dk] + b2[:, dk:2 * dk]
    b_mid = b[CHUNK // 2 - 1:CHUNK // 2, :]
    b_last = b[CHUNK - 1:CHUNK, :]
    q = p_ref[rows, h * dk:(h + 1) * dk].astype(F32)
    e_rel = jnp.exp(b - b_mid)
    q_rel = q * e_rel
    k_rel = k / e_rel
    q_abs = (q_rel * jnp.exp(b_mid)).astype(BF16)
    k_last = (k_rel * jnp.exp(b_last - b_mid)).astype(BF16)
    decay = jnp.exp(b_last)
    v = p_ref[rows, 2 * d_a + h * dk:2 * d_a + (h + 1) * dk]
    v_t = v.astype(F32).T.astype(BF16)
    scores = lax.dot_general(q_rel.astype(BF16), k_rel.astype(BF16), (((1,), (1,)), ((), ())),
                             preferred_element_type=F32)
    d_st = jnp.dot(v_t, k_last, preferred_element_type=F32)
    yield

    scores = jnp.where(causal, scores, 0.0).astype(BF16)
    st = state_ref[h]
    o = lax.dot_general(jnp.concatenate([q_abs, scores], axis=1),
                        jnp.concatenate([st.astype(BF16), v_t], axis=1),
                        (((1,), (1,)), ((), ())), preferred_element_type=F32)
    state_ref[h] = st * decay + d_st
    yield

    o = o * lax.rsqrt(jnp.mean(o * o, axis=-1, keepdims=True) + EPS) * gh_ref[...]
    z = p_ref[rows, 3 * d_a + h * dk:3 * d_a + (h + 1) * dk].astype(F32)
    y_ref[rows, hs] = (o * (z * _sigmoid(z))).astype(BF16)


MIX_UNIT_PHASES = 4


def _merge_piece(j, nc, y_ref, m_ref, wa_ref, wb_ref, mix_ref, *, d_a, d):
    cols = slice(j * nc, (j + 1) * nc)
    p_a = jnp.dot(y_ref[:, 0:d_a], wa_ref[:, cols], preferred_element_type=F32)
    p_b = jnp.dot(y_ref[:, d_a:], wb_ref[:, cols], preferred_element_type=F32)
    m_a = m_ref[:, j * nc:(j + 1) * nc].astype(F32)
    m_b = m_ref[:, d + j * nc:d + (j + 1) * nc].astype(F32)
    mix_ref[:, cols] = (m_a * p_a + m_b * p_b).astype(BF16)


def _wo_piece(j, nc, mix_ref, wo_ref, acc_ref):
    cols = slice(j * nc, (j + 1) * nc)
    acc_ref[:, cols] = jnp.dot(mix_ref[...], wo_ref[:, cols], preferred_element_type=F32)


def _post_piece(acc_ref, x_ref, mod_ref, g_ref, o_ref):
    out = acc_ref[...]
    gated_gain = mod_ref[0, 2:3, :] * g_ref[...]
    normed = out * lax.rsqrt(jnp.mean(out * out, axis=-1, keepdims=True) + EPS)
    o_ref[...] = x_ref[...] + normed * gated_gain


def _mix_out_kernel(p_ref, lbl_ref, gh_ref, cw_ref, cb_ref, m_ref, x_ref, mod_ref, wa_ref, wb_ref,
                    wo_ref, g_ref, o_ref, state_ref, u_ref, y0_ref, y1_ref, mix_ref, acc_ref,
                    *, d_a, d_b, layer, n_blocks, blocks_per_seq, out_cols):
    k = pl.program_id(0)
    blk = jnp.minimum(k, n_blocks - 1)
    ts, d = x_ref.shape
    n_chunks = ts // CHUNK
    n_out = d // out_cols

    @pl.when(blk % blocks_per_seq == 0)
    def _():
        state_ref[...] = jnp.zeros_like(state_ref)
        u_ref[0:CONV_PAD, :] = jnp.zeros((CONV_PAD, d_b), F32)

    def step(y_write, y_read):
        out_pieces = []
        if y_read is not None:
            out_pieces = (
                [functools.partial(_merge_piece, j, out_cols, y_read, m_ref, wa_ref, wb_ref, mix_ref,
                                   d_a=d_a, d=d) for j in range(n_out)]
                + [functools.partial(_wo_piece, j, out_cols, mix_ref, wo_ref, acc_ref)
                   for j in range(n_out)])
        if y_write is None:
            for piece in out_pieces:
                piece()
        else:
            lb = _lower_bound(lbl_ref, layer)
            mix_units = [[_mix_unit(c, h, p_ref, lb, gh_ref, cw_ref, cb_ref, y_write, state_ref, u_ref,
                                    d_a=d_a, d_b=d_b) for h in range(A_HEADS)]
                         for c in range(n_chunks)]
            groups = [t - ph for t in range(n_chunks + MIX_UNIT_PHASES - 1)
                      for ph in reversed(range(MIX_UNIT_PHASES)) if 0 <= t - ph < n_chunks]
            for gi, c in enumerate(groups):
                lo = (gi * len(out_pieces)) // len(groups)
                hi = ((gi + 1) * len(out_pieces)) // len(groups)
                for piece in out_pieces[lo:hi]:
                    piece()
                for unit in mix_units[c]:
                    next(unit, None)
            u_ref[0:CONV_PAD, :] = u_ref[ts:ts + CONV_PAD, :]
        if y_read is not None:
            _post_piece(acc_ref, x_ref, mod_ref, g_ref, o_ref)

    y_refs = (y0_ref, y1_ref)
    interior = jnp.logical_and(k > 0, k < n_blocks)

    @pl.when(k == 0)
    def _():
        step(y_refs[0], None)

    @pl.when(k == n_blocks)
    def _():
        step(None, y_refs[(n_blocks + 1) % 2])

    @pl.when(jnp.logical_and(interior, k % 2 == 0))
    def _():
        step(y_refs[0], y_refs[1])

    @pl.when(jnp.logical_and(interior, k % 2 == 1))
    def _():
        step(y_refs[1], y_refs[0])


def _mix_out_call(pcat, lb_logits, g_head, conv_w, conv_b, x2, mod3, w_up_a, w_up_b, w_o, g_post,
                  seq, layer, ts=256):
    t, d = x2.shape
    d_a = w_up_a.shape[0]
    d_b = w_up_b.shape[0]
    n_in = 4 * d_a + 4 * d_b
    dk = d_a // A_HEADS
    n_blocks = t // ts
    m_col_block = n_in // (2 * d)
    kern = functools.partial(_mix_out_kernel, d_a=d_a, d_b=d_b, layer=layer, n_blocks=n_blocks,
                             blocks_per_seq=seq // ts, out_cols=512)
    const = lambda k: (0, 0)
    cur = lambda k: jnp.minimum(k, n_blocks - 1)
    prev = lambda k: jnp.maximum(k - 1, 0)
    return pl.pallas_call(
        kern,
        grid=(n_blocks + 1,),
        in_specs=[
            pl.BlockSpec((ts, n_in), lambda k: (cur(k), 0)),
            pl.BlockSpec(lb_logits.shape, const),
            pl.BlockSpec((1, dk), const),
            pl.BlockSpec((CONV_W, d_b), const),
            pl.BlockSpec((1, d_b), const),
            pl.BlockSpec((ts, 2 * d), lambda k: (prev(k), m_col_block)),
            pl.BlockSpec((ts, d), lambda k: (prev(k), 0)),
            pl.BlockSpec((1, 3, d), lambda k: ((prev(k) * ts) // seq, 0, 0)),
            pl.BlockSpec((d_a, d), const, pipeline_mode=pl.Buffered(1)),
            pl.BlockSpec((d_b, d), const, pipeline_mode=pl.Buffered(1)),
            pl.BlockSpec((d, d), const, pipeline_mode=pl.Buffered(1)),
            pl.BlockSpec((1, d), const),
        ],
        out_specs=pl.BlockSpec((ts, d), lambda k: (prev(k), 0)),
        out_shape=jax.ShapeDtypeStruct((t, d), F32),
        scratch_shapes=[
            pltpu.VMEM((A_HEADS, dk, dk), F32),
            pltpu.VMEM((ts + CONV_PAD, d_b), F32),
            pltpu.VMEM((ts, d_a + d_b), BF16),
            pltpu.VMEM((ts, d_a + d_b), BF16),
            pltpu.VMEM((ts, d), BF16),
            pltpu.VMEM((ts, d), F32),
        ],
        compiler_params=pltpu.CompilerParams(
            dimension_semantics=("arbitrary",), vmem_limit_bytes=VMEM_LIMIT_BYTES),
        name="mix_merge_out",
    )(pcat, lb_logits, g_head, conv_w, conv_b, pcat, x2, mod3, w_up_a, w_up_b, w_o, g_post)


def kernel(x, c, w_ada, b_ada, g_pre, w_in, lb_logits, g_head_a, conv_w, conv_b,
           w_up_a, w_up_b, w_merge, b_merge, w_o, g_post):
    bsz, seq, d = x.shape
    depth = w_in.shape[0]
    d_a = w_up_a.shape[1]
    d_b = w_up_b.shape[1]
    n_proj = w_in.shape[2]
    assert n_proj == 4 * d_a + 4 * d_b and w_merge.shape[2] == 2 * d
    assert n_proj % (2 * d) == 0 and seq % CHUNK == 0
    x2 = x.reshape(bsz * seq, d)
    for l in range(depth):
        mod3 = _mod_call(c, w_ada[l], b_ada[l]).reshape(bsz, 3, d)
        pcat = _proj_call(x2, mod3, g_pre[l:l + 1], w_in[l].astype(BF16), w_merge[l].astype(BF16),
                          b_merge[l:l + 1], seq)
        wa16, wb16, wo16 = _cast_call([w_up_a[l], w_up_b[l], w_o[l]])
        x2 = _mix_out_call(pcat, lb_logits, g_head_a[l:l + 1], conv_w[l], conv_b[l:l + 1], x2, mod3,
                           wa16, wb16, wo16, g_post[l:l + 1], seq, l)
    return x2.reshape(bsz, seq, d)
```

```python
import functools

import jax
import jax.numpy as jnp
from jax import lax
from jax.experimental import pallas as pl
from jax.experimental.pallas import tpu as pltpu

F32 = jnp.float32
BF16 = jnp.bfloat16

EPS = 1e-6
CHUNK = 64
A_HEADS = 8
CONV_W = 3
CONV_PAD = 8

VMEM_LIMIT_BYTES = 58 * 1024 * 1024


def _sigmoid(v):
    return 1.0 / (1.0 + jnp.exp(-v))


def _mod_kernel(c_ref, w_ref, b_ref, o_ref):
    c = c_ref[...]
    c_act = c * _sigmoid(c)
    acc = jnp.dot(c_act.astype(BF16), w_ref[...].astype(BF16), preferred_element_type=F32)
    o_ref[...] = acc + b_ref[...]


def _mod_call(c, w_ada, b_ada, tn=1024):
    bsz, d = c.shape
    n = w_ada.shape[1]
    return pl.pallas_call(
        _mod_kernel,
        grid=(n // tn,),
        in_specs=[
            pl.BlockSpec((bsz, d), lambda j: (0, 0)),
            pl.BlockSpec((d, tn), lambda j: (0, j)),
            pl.BlockSpec((1, tn), lambda j: (0, j)),
        ],
        out_specs=pl.BlockSpec((bsz, tn), lambda j: (0, j)),
        out_shape=jax.ShapeDtypeStruct((bsz, n), F32),
        compiler_params=pltpu.CompilerParams(
            dimension_semantics=("arbitrary",), vmem_limit_bytes=VMEM_LIMIT_BYTES),
        name="adaln_mod",
    )(c, w_ada, b_ada.reshape(1, n))


def _cast_kernel(*refs):
    n = len(refs) // 2
    for src, dst in zip(refs[:n], refs[n:]):
        dst[...] = src[...].astype(BF16)


def _cast_call(weights, n_steps=16):
    specs = [pl.BlockSpec((w.shape[0] // n_steps, w.shape[1]), lambda s: (s, 0)) for w in weights]
    return pl.pallas_call(
        _cast_kernel,
        grid=(n_steps,),
        in_specs=specs,
        out_specs=specs,
        out_shape=[jax.ShapeDtypeStruct(w.shape, BF16) for w in weights],
        compiler_params=pltpu.CompilerParams(
            dimension_semantics=("arbitrary",), vmem_limit_bytes=VMEM_LIMIT_BYTES),
        name="cast_out_weights",
    )(*weights)


def _prenorm_rows(xs_ref, mod_ref, g_ref, h_ref, r_src, r_dst, n_rows):
    xv = xs_ref[r_src:r_src + n_rows, :]
    ms = jnp.mean(xv * xv, axis=-1, keepdims=True)
    gain = g_ref[...] * (1.0 + mod_ref[0, 1:2, :])
    h_ref[pl.ds(r_dst, n_rows), :] = ((xv * lax.rsqrt(ms + EPS)) * gain + mod_ref[0, 0:1, :]).astype(BF16)


def _proj_kernel(xs_ref, mod_ref, g_ref, wi_ref, wm_ref, b_ref, o_ref, h0_ref, h1_ref,
                 *, n_proj_tiles, sub_rows):
    i = pl.program_id(0)
    j = pl.program_id(1)
    slab_rows = xs_ref.shape[0]

    def prenorm_slab(h_ref):
        for s in range(slab_rows // sub_rows):
            r_dst = pl.multiple_of(j * slab_rows + s * sub_rows, sub_rows)
            _prenorm_rows(xs_ref, mod_ref, g_ref, h_ref, s * sub_rows, r_dst, sub_rows)

    @pl.when(jnp.logical_and(i == 0, j < n_proj_tiles))
    def _():
        prenorm_slab(h0_ref)

    def step(h_read, h_write):
        @pl.when(j < n_proj_tiles)
        def _():
            acc = jnp.dot(h_read[...], wi_ref[...], preferred_element_type=F32)
            o_ref[...] = acc.astype(BF16)
            prenorm_slab(h_write)

        @pl.when(j >= n_proj_tiles)
        def _():
            acc = jnp.dot(h_read[...], wm_ref[...], preferred_element_type=F32)
            o_ref[...] = _sigmoid(acc + b_ref[...]).astype(BF16)

    @pl.when(jnp.logical_and(i > 0, i % 2 == 1))
    def _():
        step(h0_ref, h1_ref)

    @pl.when(jnp.logical_and(i > 0, i % 2 == 0))
    def _():
        step(h1_ref, h0_ref)


def _proj_call(x2, mod3, g_pre, w_in, w_merge, b_merge, seq, tm=1024, tn=2048):
    t, d = x2.shape
    n_proj = w_in.shape[1]
    n_merge = w_merge.shape[1]
    n_i = t // tm
    n_proj_tiles = n_proj // tn
    slab = tm // n_proj_tiles
    kern = functools.partial(_proj_kernel, n_proj_tiles=n_proj_tiles, sub_rows=128)
    cur = lambda i: jnp.minimum(i, n_i - 1)
    return pl.pallas_call(
        kern,
        grid=(n_i + 1, (n_proj + n_merge) // tn),
        in_specs=[
            pl.BlockSpec((slab, d), lambda i, j: (cur(i) * n_proj_tiles + jnp.minimum(j, n_proj_tiles - 1), 0)),
            pl.BlockSpec((1, 3, d), lambda i, j: ((cur(i) * tm) // seq, 0, 0)),
            pl.BlockSpec((1, d), lambda i, j: (0, 0)),
            pl.BlockSpec((d, tn), lambda i, j: (0, jnp.where(i > 0, jnp.minimum(j, n_proj_tiles - 1), 0))),
            pl.BlockSpec((d, tn), lambda i, j: (0, jnp.where(i > 0, jnp.maximum(j - n_proj_tiles, 0), 0))),
            pl.BlockSpec((1, tn), lambda i, j: (0, jnp.maximum(j - n_proj_tiles, 0))),
        ],
        out_specs=pl.BlockSpec((tm, tn), lambda i, j: (jnp.maximum(i - 1, 0), jnp.where(i > 0, j, 0))),
        out_shape=jax.ShapeDtypeStruct((t, n_proj + n_merge), BF16),
        scratch_shapes=[pltpu.VMEM((tm, d), BF16), pltpu.VMEM((tm, d), BF16)],
        compiler_params=pltpu.CompilerParams(
            dimension_semantics=("arbitrary", "arbitrary"), vmem_limit_bytes=VMEM_LIMIT_BYTES),
        name="prenorm_proj",
    )(x2, mod3, g_pre, w_in, w_merge, b_merge)


def _lower_bound(lbl_ref, layer):
    lbl = lbl_ref[...]
    lmax = jnp.max(lbl, axis=0, keepdims=True)
    le = jnp.exp(lbl - lmax)
    return (jnp.sum(le[0:layer + 1, :], axis=0, keepdims=True)
            / jnp.sum(le, axis=0, keepdims=True))


def _causal_mask():
    row = lax.broadcasted_iota(jnp.int32, (CHUNK, CHUNK), 0)
    col = lax.broadcasted_iota(jnp.int32, (CHUNK, CHUNK), 1)
    return col <= row


def _mix_unit(c, h, p_ref, lb, gh_ref, cw_ref, cb_ref, y_ref, state_ref, u_ref, *, d_a, d_b):
    r0 = c * CHUNK
    rows = slice(r0, r0 + CHUNK)
    dk = d_a // A_HEADS
    hs = slice(h * dk, (h + 1) * dk)
    causal = _causal_mask()
    tril = causal.astype(BF16)

    f_a = p_ref[rows, d_a + h * dk:d_a + (h + 1) * dk].astype(F32)
    lbh = lb[:, hs]
    f = lbh + (1.0 - lbh) * _sigmoid(f_a)
    log_f = jnp.log(f)
    k = 1.0 - f
    lf_hi = log_f.astype(BF16)
    lf_lo = (log_f - lf_hi.astype(F32)).astype(BF16)
    b2 = jnp.dot(tril, jnp.concatenate([lf_hi, lf_lo], axis=1), preferred_element_type=F32)

    base = 4 * d_a
    pad = CONV_PAD
    gate_cc = p_ref[rows, base + d_b + h * dk:base + d_b + (h + 1) * dk].astype(F32)
    v_b = p_ref[rows, base + 2 * d_b + h * dk:base + 2 * d_b + (h + 1) * dk].astype(F32)
    u_ref[pad + r0:pad + r0 + CHUNK, hs] = gate_cc * v_b
    conv = cb_ref[:, hs] + cw_ref[CONV_W - 1:CONV_W, hs] * u_ref[pad + r0:pad + r0 + CHUNK, hs]
    for jj in range(CONV_W - 1):
        back = CONV_W - 1 - jj
        conv = conv + cw_ref[jj:jj + 1, hs] * u_ref[pad + r0 - back:pad + r0 - back + CHUNK, hs]
    gate_bb = p_ref[rows, base + h * dk:base + (h + 1) * dk].astype(F32)
    z_b = p_ref[rows, base + 3 * d_b + h * dk:base + 3 * d_b + (h + 1) * dk].astype(F32)
    y_ref[rows, d_a + h * dk:d_a + (h + 1) * dk] = (
        gate_bb * conv * (z_b * _sigmoid(z_b))).astype(BF16)
    yield

    b = b2[:, 0:dk] + b2[:, dk:2 * dk]
    b_mid = b[CHUNK // 2 - 1:CHUNK // 2, :]
    b_last = b[CHUNK - 1:CHUNK, :]
    q = p_ref[rows, h * dk:(h + 1) * dk].astype(F32)
    e_rel = jnp.exp(b - b_mid)
    q_rel = q * e_rel
    k_rel = k / e_rel
    q_abs = (q_rel * jnp.exp(b_mid)).astype(BF16)
    k_last = (k_rel * jnp.exp(b_last - b_mid)).astype(BF16)
    decay = jnp.exp(b_last)
    v = p_ref[rows, 2 * d_a + h * dk:2 * d_a + (h + 1) * dk]
    v_t = v.astype(F32).T.astype(BF16)
    scores = lax.dot_general(q_rel.astype(BF16), k_rel.astype(BF16), (((1,), (1,)), ((), ())),
                             preferred_element_type=F32)
    d_st = jnp.dot(v_t, k_last, preferred_element_type=F32)
    yield

    scores = jnp.where(causal, scores, 0.0).astype(BF16)
    st = state_ref[h]
    o = lax.dot_general(jnp.concatenate([q_abs, scores], axis=1),
                        jnp.concatenate([st.astype(BF16), v_t], axis=1),
                        (((1,), (1,)), ((), ())), preferred_element_type=F32)
    state_ref[h] = st * decay + d_st
    yield

    o = o * lax.rsqrt(jnp.mean(o * o, axis=-1, keepdims=True) + EPS) * gh_ref[...]
    z = p_ref[rows, 3 * d_a + h * dk:3 * d_a + (h + 1) * dk].astype(F32)
    y_ref[rows, hs] = (o * (z * _sigmoid(z))).astype(BF16)


MIX_UNIT_PHASES = 4


def _merge_piece(j, nc, y_ref, m_ref, wa_ref, wb_ref, mix_ref, *, d_a, d):
    cols = slice(j * nc, (j + 1) * nc)
    p_a = jnp.dot(y_ref[:, 0:d_a], wa_ref[:, cols], preferred_element_type=F32)
    p_b = jnp.dot(y_ref[:, d_a:], wb_ref[:, cols], preferred_element_type=F32)
    m_a = m_ref[:, j * nc:(j + 1) * nc].astype(F32)
    m_b = m_ref[:, d + j * nc:d + (j + 1) * nc].astype(F32)
    mix_ref[:, cols] = (m_a * p_a + m_b * p_b).astype(BF16)


def _wo_piece(j, nc, mix_ref, wo_ref, acc_ref):
    cols = slice(j * nc, (j + 1) * nc)
    acc_ref[:, cols] = jnp.dot(mix_ref[...], wo_ref[:, cols], preferred_element_type=F32)


def _post_piece(acc_ref, x_ref, mod_ref, g_ref, o_ref):
    out = acc_ref[...]
    gated_gain = mod_ref[0, 2:3, :] * g_ref[...]
    normed = out * lax.rsqrt(jnp.mean(out * out, axis=-1, keepdims=True) + EPS)
    o_ref[...] = x_ref[...] + normed * gated_gain


def _mix_out_kernel(p_ref, lbl_ref, gh_ref, cw_ref, cb_ref, m_ref, x_ref, mod_ref, wa_ref, wb_ref,
                    wo_ref, g_ref, o_ref, state_ref, u_ref, y2_ref, mix_ref, acc_ref,
                    *, d_a, d_b, layer, n_blocks, blocks_per_seq, out_cols):
    k = pl.program_id(0)
    blk = jnp.minimum(k, n_blocks - 1)
    ts, d = x_ref.shape
    n_chunks = ts // CHUNK
    n_out = d // out_cols

    @pl.when(blk % blocks_per_seq == 0)
    def _():
        state_ref[...] = jnp.zeros_like(state_ref)
        u_ref[0:CONV_PAD, :] = jnp.zeros((CONV_PAD, d_b), F32)

    def step(y_write, y_read):
        out_pieces = []
        if y_read is not None:
            out_pieces = (
                [functools.partial(_merge_piece, j, out_cols, y_read, m_ref, wa_ref, wb_ref, mix_ref,
                                   d_a=d_a, d=d) for j in range(n_out)]
                + [functools.partial(_wo_piece, j, out_cols, mix_ref, wo_ref, acc_ref)
                   for j in range(n_out)])
        if y_write is None:
            for piece in out_pieces:
                piece()
        else:
            lb = _lower_bound(lbl_ref, layer)
            mix_units = [[_mix_unit(c, h, p_ref, lb, gh_ref, cw_ref, cb_ref, y_write, state_ref, u_ref,
                                    d_a=d_a, d_b=d_b) for h in range(A_HEADS)]
                         for c in range(n_chunks)]
            groups = [t - ph for t in range(n_chunks + MIX_UNIT_PHASES - 1)
                      for ph in reversed(range(MIX_UNIT_PHASES)) if 0 <= t - ph < n_chunks]
            for gi, c in enumerate(groups):
                lo = (gi * len(out_pieces)) // len(groups)
                hi = ((gi + 1) * len(out_pieces)) // len(groups)
                for piece in out_pieces[lo:hi]:
                    piece()
                for unit in mix_units[c]:
                    next(unit, None)
            u_ref[0:CONV_PAD, :] = u_ref[ts:ts + CONV_PAD, :]
        if y_read is not None:
            _post_piece(acc_ref, x_ref, mod_ref, g_ref, o_ref)

    interior = jnp.logical_and(k > 0, k < n_blocks)

    @pl.when(k == 0)
    def _():
        step(y2_ref.at[0], None)

    @pl.when(k == n_blocks)
    def _():
        step(None, y2_ref.at[(n_blocks + 1) % 2])

    @pl.when(interior)
    def _():
        step(y2_ref.at[k % 2], y2_ref.at[(k + 1) % 2])


def _mix_out_call(pcat, lb_logits, g_head, conv_w, conv_b, x2, mod3, w_up_a, w_up_b, w_o, g_post,
                  seq, layer, ts=256):
    t, d = x2.shape
    d_a = w_up_a.shape[0]
    d_b = w_up_b.shape[0]
    n_in = 4 * d_a + 4 * d_b
    dk = d_a // A_HEADS
    n_blocks = t // ts
    m_col_block = n_in // (2 * d)
    kern = functools.partial(_mix_out_kernel, d_a=d_a, d_b=d_b, layer=layer, n_blocks=n_blocks,
                             blocks_per_seq=seq // ts, out_cols=512)
    const = lambda k: (0, 0)
    cur = lambda k: jnp.minimum(k, n_blocks - 1)
    prev = lambda k: jnp.maximum(k - 1, 0)
    return pl.pallas_call(
        kern,
        grid=(n_blocks + 1,),
        in_specs=[
            pl.BlockSpec((ts, n_in), lambda k: (cur(k), 0)),
            pl.BlockSpec(lb_logits.shape, const),
            pl.BlockSpec((1, dk), const),
            pl.BlockSpec((CONV_W, d_b), const),
            pl.BlockSpec((1, d_b), const),
            pl.BlockSpec((ts, 2 * d), lambda k: (prev(k), m_col_block)),
            pl.BlockSpec((ts, d), lambda k: (prev(k), 0)),
            pl.BlockSpec((1, 3, d), lambda k: ((prev(k) * ts) // seq, 0, 0)),
            pl.BlockSpec((d_a, d), const, pipeline_mode=pl.Buffered(1)),
            pl.BlockSpec((d_b, d), const, pipeline_mode=pl.Buffered(1)),
            pl.BlockSpec((d, d), const, pipeline_mode=pl.Buffered(1)),
            pl.BlockSpec((1, d), const),
        ],
        out_specs=pl.BlockSpec((ts, d), lambda k: (prev(k), 0)),
        out_shape=jax.ShapeDtypeStruct((t, d), F32),
        scratch_shapes=[
            pltpu.VMEM((A_HEADS, dk, dk), F32),
            pltpu.VMEM((ts + CONV_PAD, d_b), F32),
            pltpu.VMEM((2, ts, d_a + d_b), BF16),
            pltpu.VMEM((ts, d), BF16),
            pltpu.VMEM((ts, d), F32),
        ],
        compiler_params=pltpu.CompilerParams(
            dimension_semantics=("arbitrary",), vmem_limit_bytes=VMEM_LIMIT_BYTES),
        name="mix_merge_out",
    )(pcat, lb_logits, g_head, conv_w, conv_b, pcat, x2, mod3, w_up_a, w_up_b, w_o, g_post)


def kernel(x, c, w_ada, b_ada, g_pre, w_in, lb_logits, g_head_a, conv_w, conv_b,
           w_up_a, w_up_b, w_merge, b_merge, w_o, g_post):
    bsz, seq, d = x.shape
    depth = w_in.shape[0]
    d_a = w_up_a.shape[1]
    d_b = w_up_b.shape[1]
    n_proj = w_in.shape[2]
    assert n_proj == 4 * d_a + 4 * d_b and w_merge.shape[2] == 2 * d
    assert n_proj % (2 * d) == 0 and seq % CHUNK == 0
    x2 = x.reshape(bsz * seq, d)
    for l in range(depth):
        mod3 = _mod_call(c, w_ada[l], b_ada[l]).reshape(bsz, 3, d)
        pcat = _proj_call(x2, mod3, g_pre[l:l + 1], w_in[l].astype(BF16), w_merge[l].astype(BF16),
                          b_merge[l:l + 1], seq)
        wa16, wb16, wo16 = _cast_call([w_up_a[l], w_up_b[l], w_o[l]])
        x2 = _mix_out_call(pcat, lb_logits, g_head_a[l:l + 1], conv_w[l], conv_b[l:l + 1], x2, mod3,
                           wa16, wb16, wo16, g_post[l:l + 1], seq, l)
    return x2.reshape(bsz, seq, d)
```

```python
import functools

import jax
import jax.numpy as jnp
from jax import lax
from jax.experimental import pallas as pl
from jax.experimental.pallas import tpu as pltpu

F32 = jnp.float32
BF16 = jnp.bfloat16

EPS = 1e-6
CHUNK = 64
A_HEADS = 8
CONV_W = 3
CONV_PAD = 8

VMEM_LIMIT_BYTES = 58 * 1024 * 1024


def _sigmoid(v):
    return 1.0 / (1.0 + jnp.exp(-v))


def _mod_kernel(c_ref, w_ref, b_ref, o_ref):
    c = c_ref[...]
    c_act = c * _sigmoid(c)
    acc = jnp.dot(c_act.astype(BF16), w_ref[...].astype(BF16), preferred_element_type=F32)
    o_ref[...] = acc + b_ref[...]


def _mod_call(c, w_ada, b_ada, tn=1024):
    bsz, d = c.shape
    n = w_ada.shape[1]
    return pl.pallas_call(
        _mod_kernel,
        grid=(n // tn,),
        in_specs=[
            pl.BlockSpec((bsz, d), lambda j: (0, 0)),
            pl.BlockSpec((d, tn), lambda j: (0, j)),
            pl.BlockSpec((1, tn), lambda j: (0, j)),
        ],
        out_specs=pl.BlockSpec((bsz, tn), lambda j: (0, j)),
        out_shape=jax.ShapeDtypeStruct((bsz, n), F32),
        compiler_params=pltpu.CompilerParams(
            dimension_semantics=("arbitrary",), vmem_limit_bytes=VMEM_LIMIT_BYTES),
        name="adaln_mod",
    )(c, w_ada, b_ada.reshape(1, n))


def _cast_kernel(*refs):
    n = len(refs) // 2
    for src, dst in zip(refs[:n], refs[n:]):
        dst[...] = src[...].astype(BF16)


def _cast_call(weights, n_steps=16):
    specs = [pl.BlockSpec((w.shape[0] // n_steps, w.shape[1]), lambda s: (s, 0)) for w in weights]
    return pl.pallas_call(
        _cast_kernel,
        grid=(n_steps,),
        in_specs=specs,
        out_specs=specs,
        out_shape=[jax.ShapeDtypeStruct(w.shape, BF16) for w in weights],
        compiler_params=pltpu.CompilerParams(
            dimension_semantics=("arbitrary",), vmem_limit_bytes=VMEM_LIMIT_BYTES),
        name="cast_out_weights",
    )(*weights)


def _prenorm_rows(xs_ref, mod_ref, g_ref, h_ref, r_src, r_dst, n_rows):
    xv = xs_ref[r_src:r_src + n_rows, :]
    ms = jnp.mean(xv * xv, axis=-1, keepdims=True)
    gain = g_ref[...] * (1.0 + mod_ref[0, 1:2, :])
    h_ref[pl.ds(r_dst, n_rows), :] = ((xv * lax.rsqrt(ms + EPS)) * gain + mod_ref[0, 0:1, :]).astype(BF16)


def _proj_kernel(xs_ref, mod_ref, g_ref, wi_ref, wm_ref, b_ref, o_ref, h0_ref, h1_ref,
                 *, n_proj_tiles, sub_rows):
    i = pl.program_id(0)
    j = pl.program_id(1)
    slab_rows = xs_ref.shape[0]

    def prenorm_slab(h_ref):
        for s in range(slab_rows // sub_rows):
            r_dst = pl.multiple_of(j * slab_rows + s * sub_rows, sub_rows)
            _prenorm_rows(xs_ref, mod_ref, g_ref, h_ref, s * sub_rows, r_dst, sub_rows)

    @pl.when(jnp.logical_and(i == 0, j < n_proj_tiles))
    def _():
        prenorm_slab(h0_ref)

    def step(h_read, h_write):
        @pl.when(j < n_proj_tiles)
        def _():
            acc = jnp.dot(h_read[...], wi_ref[...], preferred_element_type=F32)
            o_ref[...] = acc.astype(BF16)
            prenorm_slab(h_write)

        @pl.when(j >= n_proj_tiles)
        def _():
            acc = jnp.dot(h_read[...], wm_ref[...], preferred_element_type=F32)
            o_ref[...] = _sigmoid(acc + b_ref[...]).astype(BF16)

    @pl.when(jnp.logical_and(i > 0, i % 2 == 1))
    def _():
        step(h0_ref, h1_ref)

    @pl.when(jnp.logical_and(i > 0, i % 2 == 0))
    def _():
        step(h1_ref, h0_ref)


def _proj_call(x2, mod3, g_pre, w_in, w_merge, b_merge, seq, tm=1024, tn=2048):
    t, d = x2.shape
    n_proj = w_in.shape[1]
    n_merge = w_merge.shape[1]
    n_i = t // tm
    n_proj_tiles = n_proj // tn
    slab = tm // n_proj_tiles
    kern = functools.partial(_proj_kernel, n_proj_tiles=n_proj_tiles, sub_rows=128)
    cur = lambda i: jnp.minimum(i, n_i - 1)
    return pl.pallas_call(
        kern,
        grid=(n_i + 1, (n_proj + n_merge) // tn),
        in_specs=[
            pl.BlockSpec((slab, d), lambda i, j: (cur(i) * n_proj_tiles + jnp.minimum(j, n_proj_tiles - 1), 0)),
            pl.BlockSpec((1, 3, d), lambda i, j: ((cur(i) * tm) // seq, 0, 0)),
            pl.BlockSpec((1, d), lambda i, j: (0, 0)),
            pl.BlockSpec((d, tn), lambda i, j: (0, jnp.where(i > 0, jnp.minimum(j, n_proj_tiles - 1), 0))),
            pl.BlockSpec((d, tn), lambda i, j: (0, jnp.where(i > 0, jnp.maximum(j - n_proj_tiles, 0), 0))),
            pl.BlockSpec((1, tn), lambda i, j: (0, jnp.maximum(j - n_proj_tiles, 0))),
        ],
        out_specs=pl.BlockSpec((tm, tn), lambda i, j: (jnp.maximum(i - 1, 0), jnp.where(i > 0, j, 0))),
        out_shape=jax.ShapeDtypeStruct((t, n_proj + n_merge), BF16),
        scratch_shapes=[pltpu.VMEM((tm, d), BF16), pltpu.VMEM((tm, d), BF16)],
        compiler_params=pltpu.CompilerParams(
            dimension_semantics=("arbitrary", "arbitrary"), vmem_limit_bytes=VMEM_LIMIT_BYTES),
        name="prenorm_proj",
    )(x2, mod3, g_pre, w_in, w_merge, b_merge)


def _lower_bound(lbl_ref, layer):
    lbl = lbl_ref[...]
    lmax = jnp.max(lbl, axis=0, keepdims=True)
    le = jnp.exp(lbl - lmax)
    return (jnp.sum(le[0:layer + 1, :], axis=0, keepdims=True)
            / jnp.sum(le, axis=0, keepdims=True))


def _causal_mask():
    row = lax.broadcasted_iota(jnp.int32, (CHUNK, CHUNK), 0)
    col = lax.broadcasted_iota(jnp.int32, (CHUNK, CHUNK), 1)
    return col <= row


def _mix_unit(c, h, p_ref, lb, gh_ref, cw_ref, cb_ref, y_ref, state_ref, u_ref, *, d_a, d_b):
    r0 = c * CHUNK
    rows = slice(r0, r0 + CHUNK)
    dk = d_a // A_HEADS
    hs = slice(h * dk, (h + 1) * dk)
    causal = _causal_mask()
    tril = causal.astype(BF16)

    f_a = p_ref[rows, d_a + h * dk:d_a + (h + 1) * dk].astype(F32)
    lbh = lb[:, hs]
    f = lbh + (1.0 - lbh) * _sigmoid(f_a)
    log_f = jnp.log(f)
    k = 1.0 - f
    lf_hi = log_f.astype(BF16)
    lf_lo = (log_f - lf_hi.astype(F32)).astype(BF16)
    b2 = jnp.dot(tril, jnp.concatenate([lf_hi, lf_lo], axis=1), preferred_element_type=F32)

    base = 4 * d_a
    pad = CONV_PAD
    gate_cc = p_ref[rows, base + d_b + h * dk:base + d_b + (h + 1) * dk].astype(F32)
    v_b = p_ref[rows, base + 2 * d_b + h * dk:base + 2 * d_b + (h + 1) * dk].astype(F32)
    u_ref[pad + r0:pad + r0 + CHUNK, hs] = gate_cc * v_b
    conv = cb_ref[:, hs] + cw_ref[CONV_W - 1:CONV_W, hs] * u_ref[pad + r0:pad + r0 + CHUNK, hs]
    for jj in range(CONV_W - 1):
        back = CONV_W - 1 - jj
        conv = conv + cw_ref[jj:jj + 1, hs] * u_ref[pad + r0 - back:pad + r0 - back + CHUNK, hs]
    gate_bb = p_ref[rows, base + h * dk:base + (h + 1) * dk].astype(F32)
    z_b = p_ref[rows, base + 3 * d_b + h * dk:base + 3 * d_b + (h + 1) * dk].astype(F32)
    y_ref[rows, d_a + h * dk:d_a + (h + 1) * dk] = (
        gate_bb * conv * (z_b * _sigmoid(z_b))).astype(BF16)
    yield

    b = b2[:, 0:dk] + b2[:, dk:2 * dk]
    b_mid = b[CHUNK // 2 - 1:CHUNK // 2, :]
    b_last = b[CHUNK - 1:CHUNK, :]
    q = p_ref[rows, h * dk:(h + 1) * dk].astype(F32)
    e_rel = jnp.exp(b - b_mid)
    q_rel = q * e_rel
    k_rel = k / e_rel
    q_abs = (q_rel * jnp.exp(b_mid)).astype(BF16)
    k_last = (k_rel * jnp.exp(b_last - b_mid)).astype(BF16)
    decay = jnp.exp(b_last)
    v = p_ref[rows, 2 * d_a + h * dk:2 * d_a + (h + 1) * dk]
    v_t = v.astype(F32).T.astype(BF16)
    scores = lax.dot_general(q_rel.astype(BF16), k_rel.astype(BF16), (((1,), (1,)), ((), ())),
                             preferred_element_type=F32)
    d_st = jnp.dot(v_t, k_last, preferred_element_type=F32)
    yield

    scores = jnp.where(causal, scores, 0.0).astype(BF16)
    st = state_ref[h]
    o = lax.dot_general(jnp.concatenate([q_abs, scores], axis=1),
                        jnp.concatenate([st.astype(BF16), v_t], axis=1),
                        (((1,), (1,)), ((), ())), preferred_element_type=F32)
    state_ref[h] = st * decay + d_st
    yield

    o = o * lax.rsqrt(jnp.mean(o * o, axis=-1, keepdims=True) + EPS) * gh_ref[...]
    z = p_ref[rows, 3 * d_a + h * dk:3 * d_a + (h + 1) * dk].astype(F32)
    y_ref[rows, hs] = (o * (z * _sigmoid(z))).astype(BF16)


MIX_UNIT_PHASES = 4


def _merge_piece(j, nc, y_ref, m_ref, wa_ref, wb_ref, mix_ref, *, d_a, d):
    cols = slice(j * nc, (j + 1) * nc)
    p_a = jnp.dot(y_ref[:, 0:d_a], wa_ref[:, cols], preferred_element_type=F32)
    p_b = jnp.dot(y_ref[:, d_a:], wb_ref[:, cols], preferred_element_type=F32)
    m_a = m_ref[:, j * nc:(j + 1) * nc].astype(F32)
    m_b = m_ref[:, d + j * nc:d + (j + 1) * nc].astype(F32)
    mix_ref[:, cols] = (m_a * p_a + m_b * p_b).astype(BF16)


def _wo_piece(j, nc, mix_ref, wo_ref, acc_ref):
    cols = slice(j * nc, (j + 1) * nc)
    acc_ref[:, cols] = jnp.dot(mix_ref[...], wo_ref[:, cols], preferred_element_type=F32)


def _post_piece(acc_ref, x_ref, mod_ref, g_ref, o_ref):
    out = acc_ref[...]
    gated_gain = mod_ref[0, 2:3, :] * g_ref[...]
    normed = out * lax.rsqrt(jnp.mean(out * out, axis=-1, keepdims=True) + EPS)
    o_ref[...] = x_ref[...] + normed * gated_gain


def _mix_out_kernel(p_ref, lbl_ref, gh_ref, cw_ref, cb_ref, m_ref, x_ref, mod_ref, wa_ref, wb_ref,
                    wo_ref, g_ref, o_ref, state_ref, u_ref, y2_ref, mix_ref, acc_ref,
                    *, d_a, d_b, layer, n_blocks, blocks_per_seq, out_cols):
    k = pl.program_id(0)
    blk = jnp.minimum(k, n_blocks - 1)
    ts, d = x_ref.shape
    n_chunks = ts // CHUNK
    n_out = d // out_cols

    @pl.when(blk % blocks_per_seq == 0)
    def _():
        state_ref[...] = jnp.zeros_like(state_ref)
        u_ref[0:CONV_PAD, :] = jnp.zeros((CONV_PAD, d_b), F32)

    def step(y_write, y_read):
        out_pieces = []
        if y_read is not None:
            out_pieces = (
                [functools.partial(_merge_piece, j, out_cols, y_read, m_ref, wa_ref, wb_ref, mix_ref,
                                   d_a=d_a, d=d) for j in range(n_out)]
                + [functools.partial(_wo_piece, j, out_cols, mix_ref, wo_ref, acc_ref)
                   for j in range(n_out)])
        if y_write is None:
            for piece in out_pieces:
                piece()
        else:
            lb = _lower_bound(lbl_ref, layer)
            mix_units = [[_mix_unit(c, h, p_ref, lb, gh_ref, cw_ref, cb_ref, y_write, state_ref, u_ref,
                                    d_a=d_a, d_b=d_b) for h in range(A_HEADS)]
                         for c in range(n_chunks)]
            groups = [t - ph for t in range(n_chunks + MIX_UNIT_PHASES - 1)
                      for ph in reversed(range(MIX_UNIT_PHASES)) if 0 <= t - ph < n_chunks]
            for gi, c in enumerate(groups):
                lo = (gi * len(out_pieces)) // len(groups)
                hi = ((gi + 1) * len(out_pieces)) // len(groups)
                for piece in out_pieces[lo:hi]:
                    piece()
                for unit in mix_units[c]:
                    next(unit, None)
            u_ref[0:CONV_PAD, :] = u_ref[ts:ts + CONV_PAD, :]
        if y_read is not None:
            _post_piece(acc_ref, x_ref, mod_ref, g_ref, o_ref)

    interior = jnp.logical_and(k > 0, k < n_blocks)

    @pl.when(k == 0)
    def _():
        step(y2_ref.at[0], None)

    @pl.when(k == n_blocks)
    def _():
        step(None, y2_ref.at[(n_blocks + 1) % 2])

    @pl.when(interior)
    def _():
        step(y2_ref.at[k % 2], y2_ref.at[(k + 1) % 2])


def _mix_out_call(pcat, lb_logits, g_head, conv_w, conv_b, x2, mod3, w_up_a, w_up_b, w_o, g_post,
                  seq, layer, ts=256):
    t, d = x2.shape
    d_a = w_up_a.shape[0]
    d_b = w_up_b.shape[0]
    n_in = 4 * d_a + 4 * d_b
    dk = d_a // A_HEADS
    n_blocks = t // ts
    m_col_block = n_in // (2 * d)
    kern = functools.partial(_mix_out_kernel, d_a=d_a, d_b=d_b, layer=layer, n_blocks=n_blocks,
                             blocks_per_seq=seq // ts, out_cols=256)
    const = lambda k: (0, 0)
    cur = lambda k: jnp.minimum(k, n_blocks - 1)
    prev = lambda k: jnp.maximum(k - 1, 0)
    return pl.pallas_call(
        kern,
        grid=(n_blocks + 1,),
        in_specs=[
            pl.BlockSpec((ts, n_in), lambda k: (cur(k), 0)),
            pl.BlockSpec(lb_logits.shape, const),
            pl.BlockSpec((1, dk), const),
            pl.BlockSpec((CONV_W, d_b), const),
            pl.BlockSpec((1, d_b), const),
            pl.BlockSpec((ts, 2 * d), lambda k: (prev(k), m_col_block)),
            pl.BlockSpec((ts, d), lambda k: (prev(k), 0)),
            pl.BlockSpec((1, 3, d), lambda k: ((prev(k) * ts) // seq, 0, 0)),
            pl.BlockSpec((d_a, d), const, pipeline_mode=pl.Buffered(1)),
            pl.BlockSpec((d_b, d), const, pipeline_mode=pl.Buffered(1)),
            pl.BlockSpec((d, d), const, pipeline_mode=pl.Buffered(1)),
            pl.BlockSpec((1, d), const),
        ],
        out_specs=pl.BlockSpec((ts, d), lambda k: (prev(k), 0)),
        out_shape=jax.ShapeDtypeStruct((t, d), F32),
        scratch_shapes=[
            pltpu.VMEM((A_HEADS, dk, dk), F32),
            pltpu.VMEM((ts + CONV_PAD, d_b), F32),
            pltpu.VMEM((2, ts, d_a + d_b), BF16),
            pltpu.VMEM((ts, d), BF16),
            pltpu.VMEM((ts, d), F32),
        ],
        compiler_params=pltpu.CompilerParams(
            dimension_semantics=("arbitrary",), vmem_limit_bytes=VMEM_LIMIT_BYTES),
        name="mix_merge_out",
    )(pcat, lb_logits, g_head, conv_w, conv_b, pcat, x2, mod3, w_up_a, w_up_b, w_o, g_post)


def kernel(x, c, w_ada, b_ada, g_pre, w_in, lb_logits, g_head_a, conv_w, conv_b,
           w_up_a, w_up_b, w_merge, b_merge, w_o, g_post):
    bsz, seq, d = x.shape
    depth = w_in.shape[0]
    d_a = w_up_a.shape[1]
    d_b = w_up_b.shape[1]
    n_proj = w_in.shape[2]
    assert n_proj == 4 * d_a + 4 * d_b and w_merge.shape[2] == 2 * d
    assert n_proj % (2 * d) == 0 and seq % CHUNK == 0
    x2 = x.reshape(bsz * seq, d)
    for l in range(depth):
        mod3 = _mod_call(c, w_ada[l], b_ada[l]).reshape(bsz, 3, d)
        pcat = _proj_call(x2, mod3, g_pre[l:l + 1], w_in[l].astype(BF16), w_merge[l].astype(BF16),
                          b_merge[l:l + 1], seq)
        wa16, wb16, wo16 = _cast_call([w_up_a[l], w_up_b[l], w_o[l]])
        x2 = _mix_out_call(pcat, lb_logits, g_head_a[l:l + 1], conv_w[l], conv_b[l:l + 1], x2, mod3,
                           wa16, wb16, wo16, g_post[l:l + 1], seq, l)
    return x2.reshape(bsz, seq, d)
```

```python
import functools

import jax
import jax.numpy as jnp
from jax import lax
from jax.experimental import pallas as pl
from jax.experimental.pallas import tpu as pltpu

F32 = jnp.float32
BF16 = jnp.bfloat16

EPS = 1e-6
CHUNK = 64
A_HEADS = 8
CONV_W = 3
CONV_PAD = 8

VMEM_LIMIT_BYTES = 58 * 1024 * 1024


def _sigmoid(v):
    return 1.0 / (1.0 + jnp.exp(-v))


def _mod_kernel(c_ref, w_ref, b_ref, o_ref):
    c = c_ref[...]
    c_act = c * _sigmoid(c)
    acc = jnp.dot(c_act.astype(BF16), w_ref[...].astype(BF16), preferred_element_type=F32)
    o_ref[...] = acc + b_ref[...]


def _mod_call(c, w_ada, b_ada, tn=1024):
    bsz, d = c.shape
    n = w_ada.shape[1]
    return pl.pallas_call(
        _mod_kernel,
        grid=(n // tn,),
        in_specs=[
            pl.BlockSpec((bsz, d), lambda j: (0, 0)),
            pl.BlockSpec((d, tn), lambda j: (0, j)),
            pl.BlockSpec((1, tn), lambda j: (0, j)),
        ],
        out_specs=pl.BlockSpec((bsz, tn), lambda j: (0, j)),
        out_shape=jax.ShapeDtypeStruct((bsz, n), F32),
        compiler_params=pltpu.CompilerParams(
            dimension_semantics=("arbitrary",), vmem_limit_bytes=VMEM_LIMIT_BYTES),
        name="adaln_mod",
    )(c, w_ada, b_ada.reshape(1, n))


def _cast_kernel(*refs):
    n = len(refs) // 2
    for src, dst in zip(refs[:n], refs[n:]):
        dst[...] = src[...].astype(BF16)


def _cast_call(weights, n_steps=16):
    specs = [pl.BlockSpec((w.shape[0] // n_steps, w.shape[1]), lambda s: (s, 0)) for w in weights]
    return pl.pallas_call(
        _cast_kernel,
        grid=(n_steps,),
        in_specs=specs,
        out_specs=specs,
        out_shape=[jax.ShapeDtypeStruct(w.shape, BF16) for w in weights],
        compiler_params=pltpu.CompilerParams(
            dimension_semantics=("arbitrary",), vmem_limit_bytes=VMEM_LIMIT_BYTES),
        name="cast_out_weights",
    )(*weights)


def _prenorm_rows(xs_ref, mod_ref, g_ref, h_ref, r_src, r_dst, n_rows):
    xv = xs_ref[r_src:r_src + n_rows, :]
    ms = jnp.mean(xv * xv, axis=-1, keepdims=True)
    gain = g_ref[...] * (1.0 + mod_ref[0, 1:2, :])
    h_ref[pl.ds(r_dst, n_rows), :] = ((xv * lax.rsqrt(ms + EPS)) * gain + mod_ref[0, 0:1, :]).astype(BF16)


def _proj_kernel(xs_ref, mod_ref, g_ref, wi_ref, wm_ref, b_ref, o_ref, h2_ref,
                 *, n_proj_tiles, sub_rows):
    i = pl.program_id(0)
    j = pl.program_id(1)
    slab_rows = xs_ref.shape[0]
    h_read = h2_ref.at[(i + 1) % 2]
    h_write = h2_ref.at[i % 2]

    def prenorm_slab(h_ref):
        for s in range(slab_rows // sub_rows):
            r_dst = pl.multiple_of(j * slab_rows + s * sub_rows, sub_rows)
            _prenorm_rows(xs_ref, mod_ref, g_ref, h_ref, s * sub_rows, r_dst, sub_rows)

    @pl.when(jnp.logical_and(i == 0, j < n_proj_tiles))
    def _():
        prenorm_slab(h2_ref.at[0])

    @pl.when(jnp.logical_and(i > 0, j < n_proj_tiles))
    def _():
        acc = jnp.dot(h_read[...], wi_ref[...], preferred_element_type=F32)
        o_ref[...] = acc.astype(BF16)
        prenorm_slab(h_write)

    @pl.when(jnp.logical_and(i > 0, j >= n_proj_tiles))
    def _():
        acc = jnp.dot(h_read[...], wm_ref[...], preferred_element_type=F32)
        o_ref[...] = _sigmoid(acc + b_ref[...]).astype(BF16)


def _proj_call(x2, mod3, g_pre, w_in, w_merge, b_merge, seq, tm=1024, tn=2048):
    t, d = x2.shape
    n_proj = w_in.shape[1]
    n_merge = w_merge.shape[1]
    n_i = t // tm
    n_proj_tiles = n_proj // tn
    slab = tm // n_proj_tiles
    kern = functools.partial(_proj_kernel, n_proj_tiles=n_proj_tiles, sub_rows=128)
    cur = lambda i: jnp.minimum(i, n_i - 1)
    return pl.pallas_call(
        kern,
        grid=(n_i + 1, (n_proj + n_merge) // tn),
        in_specs=[
            pl.BlockSpec((slab, d), lambda i, j: (cur(i) * n_proj_tiles + jnp.minimum(j, n_proj_tiles - 1), 0)),
            pl.BlockSpec((1, 3, d), lambda i, j: ((cur(i) * tm) // seq, 0, 0)),
            pl.BlockSpec((1, d), lambda i, j: (0, 0)),
            pl.BlockSpec((d, tn), lambda i, j: (0, jnp.where(i > 0, jnp.minimum(j, n_proj_tiles - 1), 0))),
            pl.BlockSpec((d, tn), lambda i, j: (0, jnp.where(i > 0, jnp.maximum(j - n_proj_tiles, 0), 0))),
            pl.BlockSpec((1, tn), lambda i, j: (0, jnp.maximum(j - n_proj_tiles, 0))),
        ],
        out_specs=pl.BlockSpec((tm, tn), lambda i, j: (jnp.maximum(i - 1, 0), jnp.where(i > 0, j, 0))),
        out_shape=jax.ShapeDtypeStruct((t, n_proj + n_merge), BF16),
        scratch_shapes=[pltpu.VMEM((2, tm, d), BF16)],
        compiler_params=pltpu.CompilerParams(
            dimension_semantics=("arbitrary", "arbitrary"), vmem_limit_bytes=VMEM_LIMIT_BYTES),
        name="prenorm_proj",
    )(x2, mod3, g_pre, w_in, w_merge, b_merge)


def _lower_bound(lbl_ref, layer):
    lbl = lbl_ref[...]
    lmax = jnp.max(lbl, axis=0, keepdims=True)
    le = jnp.exp(lbl - lmax)
    return (jnp.sum(le[0:layer + 1, :], axis=0, keepdims=True)
            / jnp.sum(le, axis=0, keepdims=True))


def _causal_mask():
    row = lax.broadcasted_iota(jnp.int32, (CHUNK, CHUNK), 0)
    col = lax.broadcasted_iota(jnp.int32, (CHUNK, CHUNK), 1)
    return col <= row


def _mix_unit(c, h, p_ref, lb, gh_ref, cw_ref, cb_ref, y_ref, state_ref, u_ref, *, d_a, d_b):
    r0 = c * CHUNK
    rows = slice(r0, r0 + CHUNK)
    dk = d_a // A_HEADS
    hs = slice(h * dk, (h + 1) * dk)
    causal = _causal_mask()
    tril = causal.astype(BF16)

    f_a = p_ref[rows, d_a + h * dk:d_a + (h + 1) * dk].astype(F32)
    lbh = lb[:, hs]
    f = lbh + (1.0 - lbh) * _sigmoid(f_a)
    log_f = jnp.log(f)
    k = 1.0 - f
    lf_hi = log_f.astype(BF16)
    lf_lo = (log_f - lf_hi.astype(F32)).astype(BF16)
    b2 = jnp.dot(tril, jnp.concatenate([lf_hi, lf_lo], axis=1), preferred_element_type=F32)

    base = 4 * d_a
    pad = CONV_PAD
    gate_cc = p_ref[rows, base + d_b + h * dk:base + d_b + (h + 1) * dk].astype(F32)
    v_b = p_ref[rows, base + 2 * d_b + h * dk:base + 2 * d_b + (h + 1) * dk].astype(F32)
    u_ref[pad + r0:pad + r0 + CHUNK, hs] = gate_cc * v_b
    conv = cb_ref[:, hs] + cw_ref[CONV_W - 1:CONV_W, hs] * u_ref[pad + r0:pad + r0 + CHUNK, hs]
    for jj in range(CONV_W - 1):
        back = CONV_W - 1 - jj
        conv = conv + cw_ref[jj:jj + 1, hs] * u_ref[pad + r0 - back:pad + r0 - back + CHUNK, hs]
    gate_bb = p_ref[rows, base + h * dk:base + (h + 1) * dk].astype(F32)
    z_b = p_ref[rows, base + 3 * d_b + h * dk:base + 3 * d_b + (h + 1) * dk].astype(F32)
    y_ref[rows, d_a + h * dk:d_a + (h + 1) * dk] = (
        gate_bb * conv * (z_b * _sigmoid(z_b))).astype(BF16)
    yield

    b = b2[:, 0:dk] + b2[:, dk:2 * dk]
    b_mid = b[CHUNK // 2 - 1:CHUNK // 2, :]
    b_last = b[CHUNK - 1:CHUNK, :]
    q = p_ref[rows, h * dk:(h + 1) * dk].astype(F32)
    e_rel = jnp.exp(b - b_mid)
    q_rel = q * e_rel
    k_rel = k / e_rel
    q_abs = (q_rel * jnp.exp(b_mid)).astype(BF16)
    k_last = (k_rel * jnp.exp(b_last - b_mid)).astype(BF16)
    decay = jnp.exp(b_last)
    v = p_ref[rows, 2 * d_a + h * dk:2 * d_a + (h + 1) * dk]
    v_t = v.astype(F32).T.astype(BF16)
    scores = lax.dot_general(q_rel.astype(BF16), k_rel.astype(BF16), (((1,), (1,)), ((), ())),
                             preferred_element_type=F32)
    d_st = jnp.dot(v_t, k_last, preferred_element_type=F32)
    yield

    scores = jnp.where(causal, scores, 0.0).astype(BF16)
    st = state_ref[h]
    o = lax.dot_general(jnp.concatenate([q_abs, scores], axis=1),
                        jnp.concatenate([st.astype(BF16), v_t], axis=1),
                        (((1,), (1,)), ((), ())), preferred_element_type=F32)
    state_ref[h] = st * decay + d_st
    yield

    o = o * lax.rsqrt(jnp.mean(o * o, axis=-1, keepdims=True) + EPS) * gh_ref[...]
    z = p_ref[rows, 3 * d_a + h * dk:3 * d_a + (h + 1) * dk].astype(F32)
    y_ref[rows, hs] = (o * (z * _sigmoid(z))).astype(BF16)


MIX_UNIT_PHASES = 4


def _merge_piece(j, nc, y_ref, m_ref, wa_ref, wb_ref, mix_ref, *, d_a, d):
    cols = slice(j * nc, (j + 1) * nc)
    p_a = jnp.dot(y_ref[:, 0:d_a], wa_ref[:, cols], preferred_element_type=F32)
    p_b = jnp.dot(y_ref[:, d_a:], wb_ref[:, cols], preferred_element_type=F32)
    m_a = m_ref[:, j * nc:(j + 1) * nc].astype(F32)
    m_b = m_ref[:, d + j * nc:d + (j + 1) * nc].astype(F32)
    mix_ref[:, cols] = (m_a * p_a + m_b * p_b).astype(BF16)


def _wo_piece(j, nc, mix_ref, wo_ref, acc_ref):
    cols = slice(j * nc, (j + 1) * nc)
    acc_ref[:, cols] = jnp.dot(mix_ref[...], wo_ref[:, cols], preferred_element_type=F32)


def _post_piece(acc_ref, x_ref, mod_ref, g_ref, o_ref):
    out = acc_ref[...]
    gated_gain = mod_ref[0, 2:3, :] * g_ref[...]
    normed = out * lax.rsqrt(jnp.mean(out * out, axis=-1, keepdims=True) + EPS)
    o_ref[...] = x_ref[...] + normed * gated_gain


def _mix_out_kernel(p_ref, lbl_ref, gh_ref, cw_ref, cb_ref, m_ref, x_ref, mod_ref, wa_ref, wb_ref,
                    wo_ref, g_ref, o_ref, state_ref, u_ref, y2_ref, mix_ref, acc_ref,
                    *, d_a, d_b, layer, n_blocks, blocks_per_seq, out_cols):
    k = pl.program_id(0)
    blk = jnp.minimum(k, n_blocks - 1)
    ts, d = x_ref.shape
    n_chunks = ts // CHUNK
    n_out = d // out_cols

    @pl.when(blk % blocks_per_seq == 0)
    def _():
        state_ref[...] = jnp.zeros_like(state_ref)
        u_ref[0:CONV_PAD, :] = jnp.zeros((CONV_PAD, d_b), F32)

    def step(y_write, y_read):
        out_pieces = []
        if y_read is not None:
            out_pieces = (
                [functools.partial(_merge_piece, j, out_cols, y_read, m_ref, wa_ref, wb_ref, mix_ref,
                                   d_a=d_a, d=d) for j in range(n_out)]
                + [functools.partial(_wo_piece, j, out_cols, mix_ref, wo_ref, acc_ref)
                   for j in range(n_out)])
        if y_write is None:
            for piece in out_pieces:
                piece()
        else:
            lb = _lower_bound(lbl_ref, layer)
            mix_units = [[_mix_unit(c, h, p_ref, lb, gh_ref, cw_ref, cb_ref, y_write, state_ref, u_ref,
                                    d_a=d_a, d_b=d_b) for h in range(A_HEADS)]
                         for c in range(n_chunks)]
            groups = [t - ph for t in range(n_chunks + MIX_UNIT_PHASES - 1)
                      for ph in reversed(range(MIX_UNIT_PHASES)) if 0 <= t - ph < n_chunks]
            for gi, c in enumerate(groups):
                lo = (gi * len(out_pieces)) // len(groups)
                hi = ((gi + 1) * len(out_pieces)) // len(groups)
                for piece in out_pieces[lo:hi]:
                    piece()
                for unit in mix_units[c]:
                    next(unit, None)
            u_ref[0:CONV_PAD, :] = u_ref[ts:ts + CONV_PAD, :]
        if y_read is not None:
            _post_piece(acc_ref, x_ref, mod_ref, g_ref, o_ref)

    interior = jnp.logical_and(k > 0, k < n_blocks)

    @pl.when(k == 0)
    def _():
        step(y2_ref.at[0], None)

    @pl.when(k == n_blocks)
    def _():
        step(None, y2_ref.at[(n_blocks + 1) % 2])

    @pl.when(interior)
    def _():
        step(y2_ref.at[k % 2], y2_ref.at[(k + 1) % 2])


def _mix_out_call(pcat, lb_logits, g_head, conv_w, conv_b, x2, mod3, w_up_a, w_up_b, w_o, g_post,
                  seq, layer, ts=256):
    t, d = x2.shape
    d_a = w_up_a.shape[0]
    d_b = w_up_b.shape[0]
    n_in = 4 * d_a + 4 * d_b
    dk = d_a // A_HEADS
    n_blocks = t // ts
    m_col_block = n_in // (2 * d)
    kern = functools.partial(_mix_out_kernel, d_a=d_a, d_b=d_b, layer=layer, n_blocks=n_blocks,
                             blocks_per_seq=seq // ts, out_cols=256)
    const = lambda k: (0, 0)
    cur = lambda k: jnp.minimum(k, n_blocks - 1)
    prev = lambda k: jnp.maximum(k - 1, 0)
    return pl.pallas_call(
        kern,
        grid=(n_blocks + 1,),
        in_specs=[
            pl.BlockSpec((ts, n_in), lambda k: (cur(k), 0)),
            pl.BlockSpec(lb_logits.shape, const),
            pl.BlockSpec((1, dk), const),
            pl.BlockSpec((CONV_W, d_b), const),
            pl.BlockSpec((1, d_b), const),
            pl.BlockSpec((ts, 2 * d), lambda k: (prev(k), m_col_block)),
            pl.BlockSpec((ts, d), lambda k: (prev(k), 0)),
            pl.BlockSpec((1, 3, d), lambda k: ((prev(k) * ts) // seq, 0, 0)),
            pl.BlockSpec((d_a, d), const, pipeline_mode=pl.Buffered(1)),
            pl.BlockSpec((d_b, d), const, pipeline_mode=pl.Buffered(1)),
            pl.BlockSpec((d, d), const, pipeline_mode=pl.Buffered(1)),
            pl.BlockSpec((1, d), const),
        ],
        out_specs=pl.BlockSpec((ts, d), lambda k: (prev(k), 0)),
        out_shape=jax.ShapeDtypeStruct((t, d), F32),
        scratch_shapes=[
            pltpu.VMEM((A_HEADS, dk, dk), F32),
            pltpu.VMEM((ts + CONV_PAD, d_b), F32),
            pltpu.VMEM((2, ts, d_a + d_b), BF16),
            pltpu.VMEM((ts, d), BF16),
            pltpu.VMEM((ts, d), F32),
        ],
        compiler_params=pltpu.CompilerParams(
            dimension_semantics=("arbitrary",), vmem_limit_bytes=VMEM_LIMIT_BYTES),
        name="mix_merge_out",
    )(pcat, lb_logits, g_head, conv_w, conv_b, pcat, x2, mod3, w_up_a, w_up_b, w_o, g_post)


def kernel(x, c, w_ada, b_ada, g_pre, w_in, lb_logits, g_head_a, conv_w, conv_b,
           w_up_a, w_up_b, w_merge, b_merge, w_o, g_post):
    bsz, seq, d = x.shape
    depth = w_in.shape[0]
    d_a = w_up_a.shape[1]
    d_b = w_up_b.shape[1]
    n_proj = w_in.shape[2]
    assert n_proj == 4 * d_a + 4 * d_b and w_merge.shape[2] == 2 * d
    assert n_proj % (2 * d) == 0 and seq % CHUNK == 0
    x2 = x.reshape(bsz * seq, d)
    for l in range(depth):
        mod3 = _mod_call(c, w_ada[l], b_ada[l]).reshape(bsz, 3, d)
        pcat = _proj_call(x2, mod3, g_pre[l:l + 1], w_in[l].astype(BF16), w_merge[l].astype(BF16),
                          b_merge[l:l + 1], seq)
        wa16, wb16, wo16 = _cast_call([w_up_a[l], w_up_b[l], w_o[l]])
        x2 = _mix_out_call(pcat, lb_logits, g_head_a[l:l + 1], conv_w[l], conv_b[l:l + 1], x2, mod3,
                           wa16, wb16, wo16, g_post[l:l + 1], seq, l)
    return x2.reshape(bsz, seq, d)
```
